```python
import math
import jax, jax.numpy as jnp
from jax import lax
import numpy as np

D_MODEL = 1024
BATCH = 8
SEQ = 4096
DEPTH = 2

N_GROUPS = 4
GROUP_WIDTH = D_MODEL // N_GROUPS
HEADS_PER_GROUP = 4
HEAD_DIM = GROUP_WIDTH // HEADS_PER_GROUP
DIFF_QK_DIM = HEAD_DIM // 2
D_FF = 4 * D_MODEL
CHUNK = 64
Q_BLOCK = 128
CONV_WIDTH = 4
NORM_EPS = 1e-6
IN_WIDTHS = (GROUP_WIDTH,) * 4 + (GROUP_WIDTH,) * 3 + (GROUP_WIDTH,) * 4 + (HEADS_PER_GROUP,) + (GROUP_WIDTH,) * 4 + (HEADS_PER_GROUP,) * 2
IN_COLS = 15 * GROUP_WIDTH + 3 * HEADS_PER_GROUP

kernel_name = "hybrid_parallel_groups_hgrn2_diff_fox_mlstm"


def _rms(x, gain):
    xf = x.astype(jnp.float32)
    y = xf * lax.rsqrt(jnp.mean(xf * xf, axis=-1, keepdims=True) + NORM_EPS)
    return y * gain.astype(jnp.float32)


def _modulate(x, gain, shift, scale):
    return _rms(x, gain) * (1.0 + scale[:, None, :]) + shift[:, None, :]


def _split(z, widths):
    idx = np.cumsum(np.array(widths))[:-1].tolist()
    return jnp.split(z, idx, axis=-1)


def _heads(t):
    B, S, W = t.shape
    return t.reshape(B, S, W // HEAD_DIM, HEAD_DIM).transpose(0, 2, 1, 3)


def _unheads(t):
    B, Hh, S, d = t.shape
    return t.transpose(0, 2, 1, 3).reshape(B, S, Hh * d)


def _to_chunks(t):
    B, Hh, S = t.shape[:3]
    t = t.reshape((B, Hh, S // CHUNK, CHUNK) + t.shape[3:])
    return jnp.moveaxis(t, 2, 0)


def _from_chunks(t):
    t = jnp.moveaxis(t, 0, 2)
    B, Hh, N, C, d = t.shape
    return t.reshape(B, Hh, N * C, d)


def _hgrn2(q, f_pre, i, g, lb, norm_gain):
    f = lb + (1.0 - lb) * jax.nn.sigmoid(f_pre)
    k = 1.0 - f
    logf = jnp.log(f)
    qc, kc, vc, lc = (_to_chunks(_heads(t)) for t in (q, k, i, logf))
    causal = jnp.tril(jnp.ones((CHUNK, CHUNK), dtype=bool))

    def step(state, inp):
        qt, kt, vt, lt = inp
        b = jnp.cumsum(lt, axis=-2)
        diff = b[..., :, None, :] - b[..., None, :, :]
        decay = jnp.exp(jnp.where(causal[:, :, None], diff, -jnp.inf))
        a = jnp.einsum('bhtd,bhtsd,bhsd->bhts', qt, decay, kt)
        o = jnp.einsum('bhts,bhsv->bhtv', a, vt) + jnp.einsum('bhtd,bhdv->bhtv', qt * jnp.exp(b), state)
        b_last = b[..., -1:, :]
        state = jnp.exp(b_last[..., 0, :])[..., None] * state + jnp.einsum('bhsd,bhsv->bhdv', kt * jnp.exp(b_last - b), vt)
        return state, o

    B = q.shape[0]
    s0 = jnp.zeros((B, HEADS_PER_GROUP, HEAD_DIM, HEAD_DIM), jnp.float32)
    _, o = lax.scan(step, s0, (qc, kc, vc, lc))
    o = _from_chunks(o).transpose(0, 2, 1, 3)
    o = _rms(o, norm_gain).reshape(q.shape)
    return o * jax.nn.silu(g)


def _diff_attention(q, k, v, qn_gain, kn_gain, lam_vecs, sub_gain, layer_idx):
    B, S, _ = q.shape
    H = HEADS_PER_GROUP
    q = _rms(q.reshape(B, S, H, 2, DIFF_QK_DIM), qn_gain).transpose(0, 2, 3, 1, 4) * DIFF_QK_DIM ** -0.5
    k = _rms(k.reshape(B, S, H, 2, DIFF_QK_DIM), kn_gain).transpose(0, 2, 3, 1, 4)
    v = _heads(v)
    lam_init = 0.8 - 0.6 * math.exp(-0.3 * layer_idx)
    lam_vecs = lam_vecs.astype(jnp.float32)
    lam = jnp.exp(jnp.sum(lam_vecs[0] * lam_vecs[1])) - jnp.exp(jnp.sum(lam_vecs[2] * lam_vecs[3])) + lam_init
    slopes = 2.0 ** (-8.0 * jnp.arange(1, H + 1, dtype=jnp.float32) / H)
    pos = jnp.arange(S)

    def block(bi):
        qb = lax.dynamic_slice_in_dim(q, bi * Q_BLOCK, Q_BLOCK, axis=3)
        tq = bi * Q_BLOCK + jnp.arange(Q_BLOCK)
        dist = tq[:, None] - pos[None, :]
        bias = -slopes[:, None, None] * dist.astype(jnp.float32)
        logits = jnp.einsum('bhcqd,bhcsd->bhcqs', qb, k) + bias[None, :, None]
        logits = jnp.where(dist >= 0, logits, -jnp.inf)
        p = jax.nn.softmax(logits, axis=-1)
        w = p[:, :, 0] - lam * p[:, :, 1]
        return jnp.einsum('bhqs,bhsv->bhqv', w, v)

    o = lax.map(block, jnp.arange(S // Q_BLOCK))
    o = jnp.moveaxis(o, 0, 2).reshape(B, H, S, HEAD_DIM)
    o = _rms(o, sub_gain) * (1.0 - lam_init)
    return _unheads(o)


def _forgetting_attention(q, k, v, g, f_pre, qn_gain, kn_gain, f_bias):
    B, S, _ = q.shape
    q = _rms(_heads(q), qn_gain) * HEAD_DIM ** -0.5
    k = _rms(_heads(k), kn_gain)
    v = _heads(v)
    logf = jax.nn.log_sigmoid(f_pre + f_bias.astype(jnp.float32))
    F = jnp.cumsum(logf, axis=1).transpose(0, 2, 1)
    pos = jnp.arange(S)

    def block(bi):
        qb = lax.dynamic_slice_in_dim(q, bi * Q_BLOCK, Q_BLOCK, axis=2)
        Fq = lax.dynamic_slice_in_dim(F, bi * Q_BLOCK, Q_BLOCK, axis=2)
        tq = bi * Q_BLOCK + jnp.arange(Q_BLOCK)
        logits = jnp.einsum('bhqd,bhsd->bhqs', qb, k) + Fq[..., :, None] - F[..., None, :]
        logits = jnp.where(tq[:, None] >= pos[None, :], logits, -jnp.inf)
        p = jax.nn.softmax(logits, axis=-1)
        return jnp.einsum('bhqs,bhsv->bhqv', p, v)

    o = lax.map(block, jnp.arange(S // Q_BLOCK))
    o = jnp.moveaxis(o, 0, 2).reshape(B, HEADS_PER_GROUP, S, HEAD_DIM)
    return _unheads(o) * jax.nn.sigmoid(g)


def _mlstm(q, k, v, o_pre, i_pre, f_pre, conv_w, i_bias, f_bias):
    B, S, G = q.shape
    qk = jnp.concatenate([q, k], axis=-1)
    qk = lax.conv_general_dilated(qk, conv_w.astype(qk.dtype)[:, None, :], window_strides=(1,),
                                  padding=[(CONV_WIDTH - 1, 0)], dimension_numbers=('NWC', 'WIO', 'NWC'),
                                  feature_group_count=2 * G)
    qk = jax.nn.silu(qk)
    q, k = qk[..., :G], qk[..., G:]
    log_i = (i_pre + i_bias.astype(jnp.float32)).transpose(0, 2, 1)
    log_f = jax.nn.log_sigmoid(f_pre + f_bias.astype(jnp.float32)).transpose(0, 2, 1)
    qc = _to_chunks(_heads(q) * HEAD_DIM ** -0.5)
    kc = _to_chunks(_heads(k))
    vc = _to_chunks(_heads(v))
    ic = _to_chunks(log_i)
    fc = _to_chunks(log_f)
    causal = jnp.tril(jnp.ones((CHUNK, CHUNK), dtype=bool))

    def step(carry, inp):
        cmem, nvec, m = carry
        qt, kt, vt, it, lft = inp
        b = jnp.cumsum(lft, axis=-1)
        log_d = jnp.where(causal, b[..., :, None] - b[..., None, :] + it[..., None, :], -jnp.inf)
        inter = b + m[..., None]
        m_t = jnp.maximum(inter, jnp.max(log_d, axis=-1))
        d = jnp.exp(log_d - m_t[..., None])
        w_inter = jnp.exp(inter - m_t)
        s = jnp.einsum('bhtd,bhsd->bhts', qt, kt) * d
        num = jnp.einsum('bhts,bhsv->bhtv', s, vt) + w_inter[..., None] * jnp.einsum('bhtd,bhdv->bhtv', qt, cmem)
        den = jnp.sum(s, axis=-1) + w_inter * jnp.einsum('bhtd,bhd->bht', qt, nvec)
        h = num / jnp.maximum(jnp.abs(den), jnp.exp(-m_t))[..., None]
        b_last = b[..., -1]
        log_w = b_last[..., None] - b + it
        m_new = jnp.maximum(b_last + m, jnp.max(log_w, axis=-1))
        w = jnp.exp(log_w - m_new[..., None])
        decay = jnp.exp(b_last + m - m_new)
        cmem = decay[..., None, None] * cmem + jnp.einsum('bhs,bhsd,bhsv->bhdv', w, kt, vt)
        nvec = decay[..., None] * nvec + jnp.einsum('bhs,bhsd->bhd', w, kt)
        return (cmem, nvec, m_new), h

    H = HEADS_PER_GROUP
    init = (jnp.zeros((B, H, HEAD_DIM, HEAD_DIM), jnp.float32),
            jnp.zeros((B, H, HEAD_DIM), jnp.float32),
            jnp.zeros((B, H), jnp.float32))
    _, h = lax.scan(step, init, (qc, kc, vc, ic, fc))
    h = _unheads(_from_chunks(h))
    return jax.nn.sigmoid(o_pre) * h


def setup_inputs(seed: int = 0) -> dict:
    key = jax.random.key(seed)
    ks = jax.random.split(key, 24)
    D, G, H, L = D_MODEL, GROUP_WIDTH, HEADS_PER_GROUP, DEPTH
    nrm = jax.random.normal
    f32 = jnp.float32
    return {
        "x": nrm(ks[0], (BATCH, SEQ, D), f32),
        "c": nrm(ks[1], (BATCH, D), f32),
        "w_ada": nrm(ks[2], (L, D, 6 * D), f32) * 0.5 * D ** -0.5,
        "b_ada": 0.01 * nrm(ks[3], (L, 6 * D), f32),
        "norm_mix_gain": 1.0 + 0.05 * nrm(ks[4], (L, D), f32),
        "norm_ff_gain": 1.0 + 0.05 * nrm(ks[5], (L, D), f32),
        "w_in": nrm(ks[6], (L, D, IN_COLS), f32) * D ** -0.5,
        "w_out": nrm(ks[7], (L, D, D), f32) * D ** -0.5,
        "hg_lb_logits": 0.5 * nrm(ks[8], (L, G), f32),
        "hg_norm_gain": 1.0 + 0.05 * nrm(ks[9], (L, HEAD_DIM), f32),
        "diff_qn_gain": 1.0 + 0.05 * nrm(ks[10], (L, DIFF_QK_DIM), f32),
        "diff_kn_gain": 1.0 + 0.05 * nrm(ks[11], (L, DIFF_QK_DIM), f32),
        "diff_lambda": 0.1 * nrm(ks[12], (L, 4, DIFF_QK_DIM), f32),
        "diff_sub_gain": 1.0 + 0.05 * nrm(ks[13], (L, HEAD_DIM), f32),
        "fox_qn_gain": 1.0 + 0.05 * nrm(ks[14], (L, HEAD_DIM), f32),
        "fox_kn_gain": 1.0 + 0.05 * nrm(ks[15], (L, HEAD_DIM), f32),
        "fox_f_bias": jnp.linspace(1.0, 4.0, H, dtype=f32)[None, :] + 0.01 * nrm(ks[16], (L, H), f32),
        "mlstm_conv": nrm(ks[17], (L, CONV_WIDTH, 2 * G), f32) * CONV_WIDTH ** -0.5,
        "mlstm_i_bias": 0.01 * nrm(ks[18], (L, H), f32),
        "mlstm_f_bias": jnp.linspace(3.0, 6.0, H, dtype=f32)[None, :] + 0.01 * nrm(ks[19], (L, H), f32),
        "w_ff1": nrm(ks[20], (L, D, D_FF), f32) * D ** -0.5,
        "w_ff2": nrm(ks[21], (L, D_FF, D), f32) * D_FF ** -0.5,
    }


def reference(x, c, w_ada, b_ada, norm_mix_gain, norm_ff_gain, w_in, w_out, hg_lb_logits, hg_norm_gain,
              diff_qn_gain, diff_kn_gain, diff_lambda, diff_sub_gain, fox_qn_gain, fox_kn_gain, fox_f_bias,
              mlstm_conv, mlstm_i_bias, mlstm_f_bias, w_ff1, w_ff2):
    p_lb = jax.nn.softmax(hg_lb_logits.astype(jnp.float32), axis=0)
    lower_bounds = jnp.cumsum(p_lb, axis=0) - p_lb[0:1]
    c_act = jax.nn.silu(c.astype(jnp.float32))
    for l in range(DEPTH):
        mod = c_act @ w_ada[l] + b_ada[l]
        shift1, scale1, gate1, shift2, scale2, gate2 = jnp.split(mod, 6, axis=-1)

        h = _modulate(x, norm_mix_gain[l], shift1, scale1)
        z = (h @ w_in[l]).astype(jnp.float32)
        (hq, hf, hi, hg,
         dq, dk, dv,
         fq, fk, fv, fg, ff,
         mq, mk, mv, mo, mi, mf) = _split(z, IN_WIDTHS)
        y_a = _hgrn2(hq, hf, hi, hg, lower_bounds[l], hg_norm_gain[l])
        y_b = _diff_attention(dq, dk, dv, diff_qn_gain[l], diff_kn_gain[l], diff_lambda[l], diff_sub_gain[l], l)
        y_c = _forgetting_attention(fq, fk, fv, fg, ff, fox_qn_gain[l], fox_kn_gain[l], fox_f_bias[l])
        y_d = _mlstm(mq, mk, mv, mo, mi, mf, mlstm_conv[l], mlstm_i_bias[l], mlstm_f_bias[l])
        y = jnp.concatenate([y_a, y_b, y_c, y_d], axis=-1) @ w_out[l]
        x = x + (gate1[:, None, :] * y).astype(x.dtype)

        h = _modulate(x, norm_ff_gain[l], shift2, scale2)
        y = jnp.square(jax.nn.relu(h @ w_ff1[l])) @ w_ff2[l]
        x = x + (gate2[:, None, :] * y).astype(x.dtype)
    return x
```

```python
import functools
import math

import numpy as np
import jax
import jax.numpy as jnp
from jax import lax
from jax.experimental import pallas as pl
from jax.experimental.pallas import tpu as pltpu

F32 = jnp.float32
BF16 = jnp.bfloat16

D_MODEL = 1024
N_GROUPS = 4
GROUP_WIDTH = D_MODEL // N_GROUPS
HEADS = 4
HEAD_DIM = GROUP_WIDTH // HEADS
DIFF_QK_DIM = HEAD_DIM // 2
D_FF = 4 * D_MODEL
CHUNK = 64
CONV_WIDTH = 4
NORM_EPS = 1e-6
N_FULL_GROUPS = 15
GATE_LANES = 128
Z_COLS = N_FULL_GROUPS * GROUP_WIDTH + GATE_LANES
GATE_ROWS = 16

(Z_HQ, Z_HF, Z_HI, Z_HG, Z_DQ, Z_DK, Z_DV, Z_FQ, Z_FK, Z_FV, Z_FG,
 Z_MQ, Z_MK, Z_MV, Z_MO) = range(N_FULL_GROUPS)
Z_GATE_BLOCK = N_FULL_GROUPS * GROUP_WIDTH // GATE_LANES
ROW_FOX_F, ROW_ML_B, ROW_ML_I = 0, 4, 8

VMEM_LIMIT = 56 * 1024 * 1024


def _params(sem):
    return pltpu.CompilerParams(dimension_semantics=sem, vmem_limit_bytes=VMEM_LIMIT)


def _in_column_permutation():
    g, h = GROUP_WIDTH, HEADS
    widths = (g,) * 4 + (g,) * 3 + (g,) * 4 + (h,) + (g,) * 4 + (h,) * 2
    starts = np.concatenate([[0], np.cumsum(widths)[:-1]])
    (hq, hf, hi, hg, dq, dk, dv, fq, fk, fv, fg, ff, mq, mk, mv, mo, mi, mf) = range(18)
    order_full = [hq, hf, hi, hg, dq, dk, dv, fq, fk, fv, fg, mq, mk, mv, mo]
    cols = [np.arange(starts[i], starts[i] + widths[i]) for i in order_full]
    gates = np.full((GATE_LANES,), -1, np.int64)
    gates[ROW_FOX_F:ROW_FOX_F + h] = np.arange(starts[ff], starts[ff] + h)
    gates[ROW_ML_B:ROW_ML_B + h] = np.arange(starts[mf], starts[mf] + h)
    gates[ROW_ML_I:ROW_ML_I + h] = np.arange(starts[mi], starts[mi] + h)
    return np.concatenate(cols + [gates])


def _split3(x):
    hi = x.astype(BF16)
    r1 = x - hi.astype(F32)
    mid = r1.astype(BF16)
    lo = (r1 - mid.astype(F32)).astype(BF16)
    return hi, mid, lo


def _split2(x):
    hi = x.astype(BF16)
    lo = (x - hi.astype(F32)).astype(BF16)
    return hi, lo


def _segment_sum(x, ones_bd):
    hi, lo = _split2(x)
    return (jnp.dot(hi, ones_bd, preferred_element_type=F32)
            + jnp.dot(lo, ones_bd, preferred_element_type=F32))


def _block_diag_ones(n, seg):
    idx = np.arange(n) // seg
    return jnp.asarray((idx[:, None] == idx[None, :]).astype(np.float32), dtype=BF16)


def _ada_kernel(c_ref, w_ref, b_ref, o_ref):
    c = c_ref[...]
    c_act = c * jax.nn.sigmoid(c)
    o_ref[0] = jnp.dot(c_act, w_ref[0], preferred_element_type=F32) + b_ref[0]


def _ada_modulation(c, w_ada, b_ada):
    depth, d, n = w_ada.shape
    batch = c.shape[0]
    bn = 1024
    return pl.pallas_call(
        _ada_kernel,
        grid=(depth, n // bn),
        in_specs=[
            pl.BlockSpec((batch, d), lambda l, j: (0, 0)),
            pl.BlockSpec((1, d, bn), lambda l, j: (l, 0, j)),
            pl.BlockSpec((1, 1, bn), lambda l, j: (l, 0, j)),
        ],
        out_specs=pl.BlockSpec((1, batch, bn), lambda l, j: (l, 0, j)),
        out_shape=jax.ShapeDtypeStruct((depth, batch, n), F32),
        compiler_params=_params(("parallel", "parallel")),
        name="ada_modulation",
    )(c, w_ada, b_ada.reshape(depth, 1, n))


def _modulated_norm(x, gain, shift, scale):
    ms = jnp.mean(x * x, axis=-1, keepdims=True)
    return x * lax.rsqrt(ms + NORM_EPS) * gain * (1.0 + scale) + shift


def _inproj_kernel(x_ref, gain_ref, shift_ref, scale_ref, w_ref, z_ref, *, col_chunk):
    h = _modulated_norm(x_ref[0], gain_ref[...], shift_ref[...], scale_ref[...]).astype(BF16)
    for j in range(0, Z_COLS, col_chunk):
        n = min(col_chunk, Z_COLS - j)
        z_ref[0, :, j:j + n] = jnp.dot(h, w_ref[:, j:j + n], preferred_element_type=F32)


def _mod_spec(layer, which):
    return pl.BlockSpec((None, None, None, 1, D_MODEL), lambda b, i: (layer, b, which, 0, 0))


def _input_projection(x, mod5, layer, gain, w_perm, tm):
    batch, seq, d = x.shape
    return pl.pallas_call(
        functools.partial(_inproj_kernel, col_chunk=512),
        grid=(batch, seq // tm),
        in_specs=[
            pl.BlockSpec((1, tm, d), lambda b, i: (b, i, 0)),
            pl.BlockSpec((1, d), lambda b, i: (0, 0)),
            _mod_spec(layer, 0),
            _mod_spec(layer, 1),
            pl.BlockSpec((d, Z_COLS), lambda b, i: (0, 0)),
        ],
        out_specs=pl.BlockSpec((1, tm, Z_COLS), lambda b, i: (b, i, 0)),
        out_shape=jax.ShapeDtypeStruct((batch, seq, Z_COLS), F32),
        compiler_params=_params(("parallel", "parallel")),
        name="input_projection",
    )(x, gain.reshape(1, d), mod5, mod5, w_perm)


def _ffn_kernel(x_ref, ya_ref, yb_ref, yc_ref, yd_ref, gate1_ref, gain_ref, shift_ref, scale_ref,
                gate2_ref, wo_ref, w1_ref, w2_ref, o_ref, *, ff_chunk):
    g = GROUP_WIDTH
    y = jnp.dot(ya_ref[0], wo_ref[0:g, :], preferred_element_type=F32)
    y += jnp.dot(yb_ref[0], wo_ref[g:2 * g, :], preferred_element_type=F32)
    y += jnp.dot(yc_ref[0], wo_ref[2 * g:3 * g, :], preferred_element_type=F32)
    y += jnp.dot(yd_ref[0], wo_ref[3 * g:4 * g, :], preferred_element_type=F32)
    x1 = x_ref[0] + gate1_ref[...] * y
    h = _modulated_norm(x1, gain_ref[...], shift_ref[...], scale_ref[...]).astype(BF16)
    acc = jnp.zeros_like(x1)
    for j in range(0, D_FF, ff_chunk):
        u = jnp.dot(h, w1_ref[:, j:j + ff_chunk], preferred_element_type=F32)
        u = jnp.square(jnp.maximum(u, 0.0)).astype(BF16)
        acc += jnp.dot(u, w2_ref[j:j + ff_chunk, :], preferred_element_type=F32)
    o_ref[0] = x1 + gate2_ref[...] * acc


def _out_projection_mlp(x, ys, mod5, layer, gain, w_out, w1, w2, tm):
    batch, seq, d = x.shape
    g = GROUP_WIDTH
    y_spec = pl.BlockSpec((1, tm, g), lambda b, i: (b, i, 0))
    const = lambda b, i: (0, 0)
    return pl.pallas_call(
        functools.partial(_ffn_kernel, ff_chunk=512),
        grid=(batch, seq // tm),
        in_specs=[
            pl.BlockSpec((1, tm, d), lambda b, i: (b, i, 0)),
            y_spec, y_spec, y_spec, y_spec,
            _mod_spec(layer, 2),
            pl.BlockSpec((1, d), const),
            _mod_spec(layer, 3),
            _mod_spec(layer, 4),
            _mod_spec(layer, 5),
            pl.BlockSpec((d, d), const, pipeline_mode=pl.Buffered(1)),
            pl.BlockSpec((d, D_FF), const, pipeline_mode=pl.Buffered(1)),
            pl.BlockSpec((D_FF, d), const, pipeline_mode=pl.Buffered(1)),
        ],
        out_specs=pl.BlockSpec((1, tm, d), lambda b, i: (b, i, 0)),
        out_shape=jax.ShapeDtypeStruct((batch, seq, d), F32),
        compiler_params=_params(("parallel", "parallel")),
        name="out_projection_mlp",
    )(x, *ys, mod5, gain.reshape(1, d), mod5, mod5, mod5, w_out, w1, w2)


def _log_sigmoid(t):
    return jnp.minimum(t, 0.0) - jnp.log1p(jnp.exp(-jnp.abs(t)))


def _segment_rms(x, gain, ones_bd, seg):
    ss = _segment_sum(x * x, ones_bd)
    return x * lax.rsqrt(ss * (1.0 / seg) + NORM_EPS) * gain


def _store_heads(o_ref, y):
    yb = y.astype(BF16)
    for h in range(HEADS):
        o_ref[0, h] = yb[:, h * HEAD_DIM:(h + 1) * HEAD_DIM]


def _prep_kernel(dq_ref, dk_ref, dv_ref, fq_ref, fk_ref, fv_ref, zg_ref,
                 dqn_ref, dkn_ref, fqn_ref, fkn_ref, gbias_ref, bd32_ref, bd64_ref,
                 dq_o, dk_o, dv_o, fq_o, fk_o, fv_o, grow_o, carry_ref, *, tile):
    bd32 = bd32_ref[...]
    bd64 = bd64_ref[...]
    _store_heads(dq_o, _segment_rms(dq_ref[0], dqn_ref[...], bd32, DIFF_QK_DIM) * DIFF_QK_DIM ** -0.5)
    _store_heads(dk_o, _segment_rms(dk_ref[0], dkn_ref[...], bd32, DIFF_QK_DIM))
    _store_heads(dv_o, dv_ref[0])
    _store_heads(fq_o, _segment_rms(fq_ref[0], fqn_ref[...], bd64, HEAD_DIM) * HEAD_DIM ** -0.5)
    _store_heads(fk_o, _segment_rms(fk_ref[0], fkn_ref[...], bd64, HEAD_DIM))
    _store_heads(fv_o, fv_ref[0])

    @pl.when(pl.program_id(1) == 0)
    def _():
        carry_ref[...] = jnp.zeros_like(carry_ref)

    t = zg_ref[0].T[0:GATE_ROWS, :] + gbias_ref[...]
    row = lax.broadcasted_iota(jnp.int32, (GATE_ROWS, tile), 0)
    lane = lax.broadcasted_iota(jnp.int32, (GATE_ROWS, tile), 1)
    whole_seq = row < ROW_ML_B
    per_chunk = (row >= ROW_ML_B) & (row < ROW_ML_I)
    acc = jnp.where(whole_seq | per_chunk, _log_sigmoid(t), 0.0)
    lane_in_chunk = lane & (CHUNK - 1)
    shift = 1
    while shift < tile:
        ok = whole_seq & (lane >= shift)
        if shift < CHUNK:
            ok = ok | (per_chunk & (lane_in_chunk >= shift))
        acc = acc + jnp.where(ok, pltpu.roll(acc, shift, 1), 0.0)
        shift *= 2
    acc = acc + jnp.where(whole_seq, carry_ref[:, 0:1], 0.0)
    carry_ref[...] = jnp.broadcast_to(acc[:, tile - 1:tile], carry_ref.shape)
    grow_o[0] = jnp.where(row < ROW_ML_I, acc, t)


def _attention_prep(z, layer_params, tile):
    batch, seq, _ = z.shape
    g = GROUP_WIDTH
    dqn, dkn, fqn, fkn, gbias, bd32, bd64 = layer_params

    def zspec(block):
        return pl.BlockSpec((1, tile, g), lambda b, i: (b, i, block))

    const = lambda b, i: (0, 0)
    head_spec = pl.BlockSpec((1, HEADS, tile, HEAD_DIM), lambda b, i: (b, 0, i, 0))
    head_shape = jax.ShapeDtypeStruct((batch, HEADS, seq, HEAD_DIM), BF16)
    return pl.pallas_call(
        functools.partial(_prep_kernel, tile=tile),
        grid=(batch, seq // tile),
        in_specs=[
            zspec(Z_DQ), zspec(Z_DK), zspec(Z_DV), zspec(Z_FQ), zspec(Z_FK), zspec(Z_FV),
            pl.BlockSpec((1, tile, GATE_LANES), lambda b, i: (b, i, Z_GATE_BLOCK)),
            pl.BlockSpec((1, g), const), pl.BlockSpec((1, g), const),
            pl.BlockSpec((1, g), const), pl.BlockSpec((1, g), const),
            pl.BlockSpec((GATE_ROWS, 1), const),
            pl.BlockSpec((g, g), const), pl.BlockSpec((g, g), const),
        ],
        out_specs=[head_spec] * 6 + [pl.BlockSpec((1, GATE_ROWS, tile), lambda b, i: (b, 0, i))],
        out_shape=[head_shape] * 6 + [jax.ShapeDtypeStruct((batch, GATE_ROWS, seq), F32)],
        scratch_shapes=[pltpu.VMEM((GATE_ROWS, GATE_LANES), F32)],
        compiler_params=_params(("parallel", "arbitrary")),
        name="attention_prep",
    )(z, z, z, z, z, z, z, dqn, dkn, fqn, fkn, gbias, bd32, bd64)


def _flash_head(qq, k_ref, v_ref, h, qi, tq, nrep, bias_row, m_scr, l_scr, acc_scr):
    rows = nrep * tq
    m_scr[...] = jnp.full(m_scr.shape, -jnp.inf, F32)
    l_scr[...] = jnp.zeros(l_scr.shape, F32)
    acc_scr[...] = jnp.zeros(acc_scr.shape, F32)

    def tile(j, masked):
        start = pl.multiple_of(j * tq, tq)
        k = k_ref[0, h, pl.ds(start, tq), :]
        v = v_ref[0, h, pl.ds(start, tq), :]
        s = lax.dot_general(qq, k, (((1,), (1,)), ((), ())), preferred_element_type=F32)
        s = s + bias_row(j, start)
        if masked:
            r = lax.broadcasted_iota(jnp.int32, (rows, tq), 0) & (tq - 1)
            c = lax.broadcasted_iota(jnp.int32, (rows, tq), 1)
            s = jnp.where(c <= r, s, -jnp.inf)
        m_prev = m_scr[...]
        m_new = jnp.maximum(m_prev, jnp.max(s, axis=-1, keepdims=True))
        alpha = jnp.exp(m_prev - m_new)
        p = jnp.exp(s - m_new)
        l_scr[...] = alpha * l_scr[...] + jnp.sum(p, axis=-1, keepdims=True)
        acc_scr[...] = alpha * acc_scr[...] + jnp.dot(p.astype(BF16), v, preferred_element_type=F32)
        m_scr[...] = m_new

    def full_tile(j, carry):
        tile(j, False)
        return carry

    lax.fori_loop(0, qi, full_tile, 0)
    tile(qi, True)


def _diff_attn_kernel(slopes_ref, lam_ref, q_ref, k_ref, v_ref, gain_ref, bd64_ref, o_ref,
                      m_scr, l_scr, acc_scr, oh_scr, y_scr, *, tq, out_scale):
    qi = pl.program_id(1)
    lam = lam_ref[0]
    lane = lax.broadcasted_iota(jnp.int32, (tq, HEAD_DIM), 1)
    col = lax.broadcasted_iota(jnp.int32, (1, tq), 1)

    def head(h, carry):
        q = q_ref[0, h]
        first = jnp.where(lane < DIFF_QK_DIM, q, jnp.zeros_like(q))
        second = jnp.where(lane >= DIFF_QK_DIM, q, jnp.zeros_like(q))
        qq = jnp.concatenate([first, second], axis=0)
        slope = slopes_ref[h]

        def bias_row(j, start):
            return slope * (col + (start - qi * tq)).astype(F32)

        _flash_head(qq, k_ref, v_ref, h, qi, tq, 2, bias_row, m_scr, l_scr, acc_scr)
        o = acc_scr[...] / l_scr[...]
        oh_scr[h] = o[0:tq] - lam * o[tq:2 * tq]
        return carry

    lax.fori_loop(0, HEADS, head, 0)
    for h in range(HEADS):
        y_scr[:, h * HEAD_DIM:(h + 1) * HEAD_DIM] = oh_scr[h]
    y = _segment_rms(y_scr[...], gain_ref[...] * out_scale, bd64_ref[...], HEAD_DIM)
    o_ref[0] = y.astype(BF16)


def _fox_attn_kernel(q_ref, k_ref, v_ref, grow_ref, g_ref, o_ref,
                     m_scr, l_scr, acc_scr, oh_scr, y_scr, *, tq):
    qi = pl.program_id(1)
    q_start = pl.multiple_of(qi * tq, tq)

    def head(h, carry):
        f_first = grow_ref[0, pl.ds(ROW_FOX_F + h, 1), pl.ds(q_start, tq)][:, 0:1]

        def bias_row(j, start):
            return f_first - grow_ref[0, pl.ds(ROW_FOX_F + h, 1), pl.ds(start, tq)]

        _flash_head(q_ref[0, h], k_ref, v_ref, h, qi, tq, 1, bias_row, m_scr, l_scr, acc_scr)
        oh_scr[h] = acc_scr[...] / l_scr[...]
        return carry

    lax.fori_loop(0, HEADS, head, 0)
    for h in range(HEADS):
        y_scr[:, h * HEAD_DIM:(h + 1) * HEAD_DIM] = oh_scr[h]
    o_ref[0] = (y_scr[...] * jax.nn.sigmoid(g_ref[0])).astype(BF16)


def _attn_scratch(tq, nrep):
    return [
        pltpu.VMEM((nrep * tq, 1), F32),
        pltpu.VMEM((nrep * tq, 1), F32),
        pltpu.VMEM((nrep * tq, HEAD_DIM), F32),
        pltpu.VMEM((HEADS, tq, HEAD_DIM), F32),
        pltpu.VMEM((tq, GROUP_WIDTH), F32),
    ]


def _diff_attention(q, k, v, slopes, lam, gain, bd64, out_scale, tq):
    batch, _, seq, _ = q.shape
    g = GROUP_WIDTH
    kv_spec = pl.BlockSpec((1, HEADS, seq, HEAD_DIM), lambda b, i: (b, 0, 0, 0))
    smem = pl.BlockSpec(memory_space=pltpu.SMEM)
    return pl.pallas_call(
        functools.partial(_diff_attn_kernel, tq=tq, out_scale=out_scale),
        grid=(batch, seq // tq),
        in_specs=[
            smem, smem,
            pl.BlockSpec((1, HEADS, tq, HEAD_DIM), lambda b, i: (b, 0, i, 0)),
            kv_spec, kv_spec,
            pl.BlockSpec((1, g), lambda b, i: (0, 0)),
            pl.BlockSpec((g, g), lambda b, i: (0, 0)),
        ],
        out_specs=pl.BlockSpec((1, tq, g), lambda b, i: (b, i, 0)),
        out_shape=jax.ShapeDtypeStruct((batch, seq, g), BF16),
        scratch_shapes=_attn_scratch(tq, 2),
        compiler_params=_params(("parallel", "arbitrary")),
        name="diff_attention",
    )(slopes, lam, q, k, v, gain, bd64)


def _fox_attention(q, k, v, grow, z, tq):
    batch, _, seq, _ = q.shape
    g = GROUP_WIDTH
    kv_spec = pl.BlockSpec((1, HEADS, seq, HEAD_DIM), lambda b, i: (b, 0, 0, 0))
    return pl.pallas_call(
        functools.partial(_fox_attn_kernel, tq=tq),
        grid=(batch, seq // tq),
        in_specs=[
            pl.BlockSpec((1, HEADS, tq, HEAD_DIM), lambda b, i: (b, 0, i, 0)),
            kv_spec, kv_spec,
            pl.BlockSpec((1, GATE_ROWS, seq), lambda b, i: (b, 0, 0)),
            pl.BlockSpec((1, tq, g), lambda b, i: (b, i, Z_FG)),
        ],
        out_specs=pl.BlockSpec((1, tq, g), lambda b, i: (b, i, 0)),
        out_shape=jax.ShapeDtypeStruct((batch, seq, g), BF16),
        scratch_shapes=_attn_scratch(tq, 1),
        compiler_params=_params(("parallel", "arbitrary")),
        name="fox_attention",
    )(q, k, v, grow, z)


def _hgrn2_kernel(q_ref, f_ref, i_ref, g_ref, lb_ref, gain_ref, bd64_ref, tri_ref, mask_ref,
                  o_ref, st_ref, *, chunks):
    @pl.when(pl.program_id(1) == 0)
    def _():
        st_ref[...] = jnp.zeros_like(st_ref)

    lb = lb_ref[...]
    bd64 = bd64_ref[...]
    tri = tri_ref[...]
    sub = 8
    row8 = lax.broadcasted_iota(jnp.int32, (sub, GROUP_WIDTH), 0)

    def chunk(c, carry):
        rows = pl.ds(pl.multiple_of(c * CHUNK, CHUNK), CHUNK)
        q = q_ref[0, rows, :]
        f = lb + (1.0 - lb) * jax.nn.sigmoid(f_ref[0, rows, :])
        kk = 1.0 - f
        v = i_ref[0, rows, :]
        hi, mid, lo = _split3(jnp.log(f))
        b = (jnp.dot(tri, hi, preferred_element_type=F32)
             + jnp.dot(tri, mid, preferred_element_type=F32)
             + jnp.dot(tri, lo, preferred_element_type=F32))

        o_rows = [jnp.zeros((sub, GROUP_WIDTH), F32) for _ in range(CHUNK // sub)]
        for lag in range(CHUNK):
            r8 = lag // sub * sub
            if lag == 0:
                x = q * kk
                vd = v
            else:
                kd = pltpu.roll(kk, lag, 0)[r8:]
                bd = pltpu.roll(b, lag, 0)[r8:]
                vd = pltpu.roll(v, lag, 0)[r8:]
                x = q[r8:] * kd * jnp.exp(b[r8:] - bd)
                if lag % sub:
                    head = jnp.where(row8 >= lag - r8, x[0:sub], 0.0)
                    x = jnp.concatenate([head, x[sub:]], axis=0) if r8 + sub < CHUNK else head
            a = jnp.dot(x.astype(BF16), bd64, preferred_element_type=F32)
            contrib = a * vd
            for blk in range(r8 // sub, CHUNK // sub):
                lo_r = blk * sub - r8
                o_rows[blk] = o_rows[blk] + contrib[lo_r:lo_r + sub]
        o = jnp.concatenate(o_rows, axis=0)

        st = st_ref[...]
        qe = (q * jnp.exp(b)).astype(BF16)
        o = o + lax.dot_general(qe, st.astype(BF16), (((1,), (1,)), ((), ())),
                                preferred_element_type=F32)
        b_last = b[CHUNK - 1:CHUNK, :]
        kp = (kk * jnp.exp(b_last - b)).astype(BF16)
        upd = lax.dot_general(v.astype(BF16), kp, (((0,), (0,)), ((), ())),
                              preferred_element_type=F32)
        st_ref[...] = mask_ref[...] * (st * jnp.exp(b_last) + upd)

        y = _segment_rms(o, gain_ref[...], bd64, HEAD_DIM)
        g = g_ref[0, rows, :]
        o_ref[0, rows, :] = (y * (g * jax.nn.sigmoid(g))).astype(BF16)
        return carry

    lax.fori_loop(0, chunks, chunk, 0)


def _hgrn2(z, lb, gain, bd64, tri, mask, tile):
    batch, seq, _ = z.shape
    g = GROUP_WIDTH

    def zspec(block):
        return pl.BlockSpec((1, tile, g), lambda b, i: (b, i, block))

    const = lambda b, i: (0, 0)
    return pl.pallas_call(
        functools.partial(_hgrn2_kernel, chunks=tile // CHUNK),
        grid=(batch, seq // tile),
        in_specs=[
            zspec(Z_HQ), zspec(Z_HF), zspec(Z_HI), zspec(Z_HG),
            pl.BlockSpec((1, g), const), pl.BlockSpec((1, g), const),
            pl.BlockSpec((g, g), const), pl.BlockSpec((CHUNK, CHUNK), const),
            pl.BlockSpec((g, g), const),
        ],
        out_specs=pl.BlockSpec((1, tile, g), lambda b, i: (b, i, 0)),
        out_shape=jax.ShapeDtypeStruct((batch, seq, g), BF16),
        scratch_shapes=[pltpu.VMEM((g, g), F32)],
        compiler_params=_params(("parallel", "arbitrary")),
        name="hgrn2",
    )(z, z, z, z, lb, gain, bd64, tri, mask)


def _mlstm_kernel(q_ref, k_ref, v_ref, og_ref, gcol_ref, grow_ref, conv_ref, y_ref,
                  prev_scr, c_scr, n_scr, m_scr, y_scr, *, tile):
    @pl.when(pl.program_id(1) == 0)
    def _():
        prev_scr[...] = jnp.zeros_like(prev_scr)
        c_scr[...] = jnp.zeros_like(c_scr)
        n_scr[...] = jnp.zeros_like(n_scr)
        m_scr[...] = jnp.zeros_like(m_scr)

    g = GROUP_WIDTH
    halo = prev_scr.shape[0]
    raw = jnp.concatenate([q_ref[0], k_ref[0]], axis=1)
    padded = jnp.concatenate([prev_scr[...], raw], axis=0)
    w = conv_ref[...]
    conv = w[CONV_WIDTH - 1:CONV_WIDTH, :] * raw
    for back in range(1, CONV_WIDTH):
        tap = CONV_WIDTH - 1 - back
        conv = conv + w[tap:tap + 1, :] * pltpu.roll(padded, back, 0)[halo:]
    prev_scr[...] = raw[tile - halo:]
    act = conv * jax.nn.sigmoid(conv)
    qc = act[:, :g] * HEAD_DIM ** -0.5
    kc = act[:, g:]
    v = v_ref[0]

    r_idx = lax.broadcasted_iota(jnp.int32, (CHUNK, CHUNK), 0)
    c_idx = lax.broadcasted_iota(jnp.int32, (CHUNK, CHUNK), 1)
    causal = c_idx <= r_idx
    for c in range(tile // CHUNK):
        r0 = c * CHUNK
        for h in range(HEADS):
            hs = slice(h * HEAD_DIM, (h + 1) * HEAD_DIM)
            qh = qc[r0:r0 + CHUNK, hs]
            kh = kc[r0:r0 + CHUNK, hs]
            vh = v[r0:r0 + CHUNK, hs].astype(BF16)
            b_col = gcol_ref[0, r0:r0 + CHUNK, ROW_ML_B + h:ROW_ML_B + h + 1]
            i_col = gcol_ref[0, r0:r0 + CHUNK, ROW_ML_I + h:ROW_ML_I + h + 1]
            b_row = grow_ref[0, ROW_ML_B + h:ROW_ML_B + h + 1, r0:r0 + CHUNK]
            i_row = grow_ref[0, ROW_ML_I + h:ROW_ML_I + h + 1, r0:r0 + CHUNK]
            m = m_scr[h][0:1, 0:1]
            cm = c_scr[h]
            nv = n_scr[h]

            log_d = jnp.where(causal, b_col - b_row + i_row, -jnp.inf)
            inter = b_col + m
            m_t = jnp.maximum(inter, jnp.max(log_d, axis=-1, keepdims=True))
            dmat = jnp.exp(log_d - m_t)
            w_inter = jnp.exp(inter - m_t)
            qb = qh.astype(BF16)
            s = lax.dot_general(qb, kh.astype(BF16), (((1,), (1,)), ((), ())),
                                preferred_element_type=F32) * dmat
            num = (jnp.dot(s.astype(BF16), vh, preferred_element_type=F32)
                   + w_inter * jnp.dot(qb, cm.astype(BF16), preferred_element_type=F32))
            den = (jnp.sum(s, axis=-1, keepdims=True)
                   + w_inter * jnp.sum(qh * nv, axis=-1, keepdims=True))
            y_scr[r0:r0 + CHUNK, hs] = num / jnp.maximum(jnp.abs(den), jnp.exp(-m_t))

            b_last = b_col[CHUNK - 1:CHUNK, :]
            log_w = b_last - b_col + i_col
            m_new = jnp.maximum(b_last + m, jnp.max(log_w, axis=0, keepdims=True))
            decay = jnp.exp(b_last + m - m_new)
            kw = kh * jnp.exp(log_w - m_new)
            c_scr[h] = decay * cm + lax.dot_general(kw.astype(BF16), vh, (((0,), (0,)), ((), ())),
                                                    preferred_element_type=F32)
            n_scr[h] = decay * nv + jnp.sum(kw, axis=0, keepdims=True)
            m_scr[h] = jnp.broadcast_to(m_new, m_scr.shape[1:])

    y_ref[0] = (jax.nn.sigmoid(og_ref[0]) * y_scr[...]).astype(BF16)


def _mlstm(z, gcol, grow, conv_w, tile):
    batch, seq, _ = z.shape
    g = GROUP_WIDTH

    def zspec(block):
        return pl.BlockSpec((1, tile, g), lambda b, i: (b, i, block))

    return pl.pallas_call(
        functools.partial(_mlstm_kernel, tile=tile),
        grid=(batch, seq // tile),
        in_specs=[
            zspec(Z_MQ), zspec(Z_MK), zspec(Z_MV), zspec(Z_MO),
            pl.BlockSpec((1, tile, GATE_ROWS), lambda b, i: (b, i, 0)),
            pl.BlockSpec((1, GATE_ROWS, tile), lambda b, i: (b, 0, i)),
            pl.BlockSpec((CONV_WIDTH, 2 * g), lambda b, i: (0, 0)),
        ],
        out_specs=pl.BlockSpec((1, tile, g), lambda b, i: (b, i, 0)),
        out_shape=jax.ShapeDtypeStruct((batch, seq, g), BF16),
        scratch_shapes=[
            pltpu.VMEM((8, 2 * g), F32),
            pltpu.VMEM((HEADS, HEAD_DIM, HEAD_DIM), F32),
            pltpu.VMEM((HEADS, 1, HEAD_DIM), F32),
            pltpu.VMEM((HEADS, 8, 128), F32),
            pltpu.VMEM((tile, g), F32),
        ],
        compiler_params=_params(("parallel", "arbitrary")),
        name="mlstm",
    )(z, z, z, z, gcol, grow, conv_w)


def _tile_rows(v, reps):
    return jnp.tile(v.astype(F32), reps).reshape(1, -1)


def kernel(x, c, w_ada, b_ada, norm_mix_gain, norm_ff_gain, w_in, w_out, hg_lb_logits, hg_norm_gain,
           diff_qn_gain, diff_kn_gain, diff_lambda, diff_sub_gain, fox_qn_gain, fox_kn_gain, fox_f_bias,
           mlstm_conv, mlstm_i_bias, mlstm_f_bias, w_ff1, w_ff2):
    depth = w_in.shape[0]
    batch, seq, d = x.shape
    g, hd = GROUP_WIDTH, HEAD_DIM
    tm = min(512, seq)
    tq = min(256, seq)
    tr = min(256, seq)

    p_lb = jax.nn.softmax(hg_lb_logits.astype(F32), axis=0)
    lower_bounds = jnp.cumsum(p_lb, axis=0) - p_lb[0:1]
    perm = _in_column_permutation()
    perm_idx = jnp.asarray(np.maximum(perm, 0), jnp.int32)
    perm_valid = jnp.asarray(perm >= 0)
    bd32 = _block_diag_ones(g, DIFF_QK_DIM)
    bd64 = _block_diag_ones(g, hd)
    bd64_f32 = bd64.astype(F32)
    tri = jnp.asarray(np.tril(np.ones((CHUNK, CHUNK), np.float32)), dtype=BF16)
    slopes = jnp.asarray(2.0 ** (-8.0 * np.arange(1, HEADS + 1) / HEADS), F32)

    mod = _ada_modulation(c.astype(F32), w_ada, b_ada)
    mod5 = mod.reshape(depth, batch, 6, 1, d)

    for l in range(depth):
        w_perm = jnp.where(perm_valid[None, :], jnp.take(w_in[l], perm_idx, axis=1), 0.0).astype(BF16)
        z = _input_projection(x, mod5, l, norm_mix_gain[l], w_perm, tm)

        gbias = jnp.zeros((GATE_ROWS,), F32)
        gbias = gbias.at[ROW_FOX_F:ROW_FOX_F + HEADS].set(fox_f_bias[l].astype(F32))
        gbias = gbias.at[ROW_ML_B:ROW_ML_B + HEADS].set(mlstm_f_bias[l].astype(F32))
        gbias = gbias.at[ROW_ML_I:ROW_ML_I + HEADS].set(mlstm_i_bias[l].astype(F32))
        prep_params = (
            _tile_rows(diff_qn_gain[l], 2 * HEADS), _tile_rows(diff_kn_gain[l], 2 * HEADS),
            _tile_rows(fox_qn_gain[l], HEADS), _tile_rows(fox_kn_gain[l], HEADS),
            gbias.reshape(GATE_ROWS, 1), bd32, bd64,
        )
        dq, dk, dv, fq, fk, fv, grow = _attention_prep(z, prep_params, tm)
        gcol = jnp.transpose(grow, (0, 2, 1))

        lam_init = 0.8 - 0.6 * math.exp(-0.3 * l)
        lv = diff_lambda[l].astype(F32)
        lam = jnp.exp(jnp.sum(lv[0] * lv[1])) - jnp.exp(jnp.sum(lv[2] * lv[3])) + lam_init

        y_a = _hgrn2(z, lower_bounds[l].reshape(1, g), _tile_rows(hg_norm_gain[l], HEADS),
                     bd64, tri, bd64_f32, tr)
        y_b = _diff_attention(dq, dk, dv, slopes, lam.reshape(1), _tile_rows(diff_sub_gain[l], HEADS),
                              bd64, 1.0 - lam_init, tq)
        y_c = _fox_attention(fq, fk, fv, grow, z, tq)
        y_d = _mlstm(z, gcol, grow, mlstm_conv[l].astype(F32), tr)

        x = _out_projection_mlp(x, (y_a, y_b, y_c, y_d), mod5, l, norm_ff_gain[l],
                                w_out[l].astype(BF16), w_ff1[l].astype(BF16), w_ff2[l].astype(BF16), tm)
    return x
```

```python
import functools
import math

import numpy as np
import jax
import jax.numpy as jnp
from jax import lax
from jax.experimental import pallas as pl
from jax.experimental.pallas import tpu as pltpu

F32 = jnp.float32
BF16 = jnp.bfloat16

D_MODEL = 1024
N_GROUPS = 4
GROUP_WIDTH = D_MODEL // N_GROUPS
HEADS = 4
HEAD_DIM = GROUP_WIDTH // HEADS
DIFF_QK_DIM = HEAD_DIM // 2
D_FF = 4 * D_MODEL
CHUNK = 64
CONV_WIDTH = 4
NORM_EPS = 1e-6
N_FULL_GROUPS = 15
GATE_LANES = 128
Z_COLS = N_FULL_GROUPS * GROUP_WIDTH + GATE_LANES
GATE_ROWS = 16
KEY_WIDTH = 128
LOG2E = math.log2(math.e)

(Z_HQ, Z_HF, Z_HI, Z_HG, Z_DQ, Z_DK, Z_DV, Z_FQ, Z_FK, Z_FV, Z_FG,
 Z_MQ, Z_MK, Z_MV, Z_MO) = range(N_FULL_GROUPS)
Z_GATE_BLOCK = N_FULL_GROUPS * GROUP_WIDTH // GATE_LANES
ROW_FOX_F, ROW_ML_B, ROW_ML_I = 0, 4, 8

VMEM_LIMIT = 56 * 1024 * 1024


def _params(sem):
    return pltpu.CompilerParams(dimension_semantics=sem, vmem_limit_bytes=VMEM_LIMIT)


def _in_column_permutation():
    g, h = GROUP_WIDTH, HEADS
    widths = (g,) * 4 + (g,) * 3 + (g,) * 4 + (h,) + (g,) * 4 + (h,) * 2
    starts = np.concatenate([[0], np.cumsum(widths)[:-1]])
    (hq, hf, hi, hg, dq, dk, dv, fq, fk, fv, fg, ff, mq, mk, mv, mo, mi, mf) = range(18)
    order_full = [hq, hf, hi, hg, dq, dk, dv, fq, fk, fv, fg, mq, mk, mv, mo]
    cols = [np.arange(starts[i], starts[i] + widths[i]) for i in order_full]
    gates = np.full((GATE_LANES,), -1, np.int64)
    gates[ROW_FOX_F:ROW_FOX_F + h] = np.arange(starts[ff], starts[ff] + h)
    gates[ROW_ML_B:ROW_ML_B + h] = np.arange(starts[mf], starts[mf] + h)
    gates[ROW_ML_I:ROW_ML_I + h] = np.arange(starts[mi], starts[mi] + h)
    return np.concatenate(cols + [gates])


def _split3(x):
    hi = x.astype(BF16)
    r1 = x - hi.astype(F32)
    mid = r1.astype(BF16)
    lo = (r1 - mid.astype(F32)).astype(BF16)
    return hi, mid, lo


def _split2(x):
    hi = x.astype(BF16)
    lo = (x - hi.astype(F32)).astype(BF16)
    return hi, lo


def _segment_sum(x, ones_bd):
    hi, lo = _split2(x)
    return (jnp.dot(hi, ones_bd, preferred_element_type=F32)
            + jnp.dot(lo, ones_bd, preferred_element_type=F32))


def _block_diag_ones(n, seg):
    idx = np.arange(n) // seg
    return jnp.asarray((idx[:, None] == idx[None, :]).astype(np.float32), dtype=BF16)


def _ada_kernel(c_ref, w_ref, b_ref, o_ref):
    c = c_ref[...]
    c_act = c * jax.nn.sigmoid(c)
    o_ref[0] = jnp.dot(c_act, w_ref[0], preferred_element_type=F32) + b_ref[0]


def _ada_modulation(c, w_ada, b_ada):
    depth, d, n = w_ada.shape
    batch = c.shape[0]
    bn = 1024
    return pl.pallas_call(
        _ada_kernel,
        grid=(depth, n // bn),
        in_specs=[
            pl.BlockSpec((batch, d), lambda l, j: (0, 0)),
            pl.BlockSpec((1, d, bn), lambda l, j: (l, 0, j)),
            pl.BlockSpec((1, 1, bn), lambda l, j: (l, 0, j)),
        ],
        out_specs=pl.BlockSpec((1, batch, bn), lambda l, j: (l, 0, j)),
        out_shape=jax.ShapeDtypeStruct((depth, batch, n), F32),
        compiler_params=_params(("parallel", "parallel")),
        name="ada_modulation",
    )(c, w_ada, b_ada.reshape(depth, 1, n))


def _modulated_norm(x, gain, shift, scale):
    ms = jnp.mean(x * x, axis=-1, keepdims=True)
    return x * lax.rsqrt(ms + NORM_EPS) * gain * (1.0 + scale) + shift


def _inproj_kernel(x_ref, gain_ref, shift_ref, scale_ref, w_ref, z_ref, *, col_chunk):
    h = _modulated_norm(x_ref[0], gain_ref[...], shift_ref[...], scale_ref[...]).astype(BF16)
    for j in range(0, Z_COLS, col_chunk):
        n = min(col_chunk, Z_COLS - j)
        z_ref[0, :, j:j + n] = jnp.dot(h, w_ref[:, j:j + n], preferred_element_type=F32)


def _mod_spec(layer, which):
    return pl.BlockSpec((None, None, None, 1, D_MODEL), lambda b, i: (layer, b, which, 0, 0))


def _input_projection(x, mod5, layer, gain, w_perm, tm):
    batch, seq, d = x.shape
    return pl.pallas_call(
        functools.partial(_inproj_kernel, col_chunk=512),
        grid=(batch, seq // tm),
        in_specs=[
            pl.BlockSpec((1, tm, d), lambda b, i: (b, i, 0)),
            pl.BlockSpec((1, d), lambda b, i: (0, 0)),
            _mod_spec(layer, 0),
            _mod_spec(layer, 1),
            pl.BlockSpec((d, Z_COLS), lambda b, i: (0, 0)),
        ],
        out_specs=pl.BlockSpec((1, tm, Z_COLS), lambda b, i: (b, i, 0)),
        out_shape=jax.ShapeDtypeStruct((batch, seq, Z_COLS), F32),
        compiler_params=_params(("parallel", "parallel")),
        name="input_projection",
    )(x, gain.reshape(1, d), mod5, mod5, w_perm)


def _ffn_kernel(x_ref, ya_ref, yb_ref, yc_ref, yd_ref, gate1_ref, gain_ref, shift_ref, scale_ref,
                gate2_ref, wo_ref, w1_ref, w2_ref, o_ref, *, ff_chunk):
    g = GROUP_WIDTH
    y = jnp.dot(ya_ref[0], wo_ref[0:g, :], preferred_element_type=F32)
    y += jnp.dot(yb_ref[0], wo_ref[g:2 * g, :], preferred_element_type=F32)
    y += jnp.dot(yc_ref[0], wo_ref[2 * g:3 * g, :], preferred_element_type=F32)
    y += jnp.dot(yd_ref[0], wo_ref[3 * g:4 * g, :], preferred_element_type=F32)
    x1 = x_ref[0] + gate1_ref[...] * y
    h = _modulated_norm(x1, gain_ref[...], shift_ref[...], scale_ref[...]).astype(BF16)
    acc = jnp.zeros_like(x1)
    for j in range(0, D_FF, ff_chunk):
        u = jnp.dot(h, w1_ref[:, j:j + ff_chunk], preferred_element_type=F32)
        u = jnp.square(jnp.maximum(u, 0.0)).astype(BF16)
        acc += jnp.dot(u, w2_ref[j:j + ff_chunk, :], preferred_element_type=F32)
    o_ref[0] = x1 + gate2_ref[...] * acc


def _out_projection_mlp(x, ys, mod5, layer, gain, w_out, w1, w2, tm):
    batch, seq, d = x.shape
    g = GROUP_WIDTH
    y_spec = pl.BlockSpec((1, tm, g), lambda b, i: (b, i, 0))
    const = lambda b, i: (0, 0)
    return pl.pallas_call(
        functools.partial(_ffn_kernel, ff_chunk=512),
        grid=(batch, seq // tm),
        in_specs=[
            pl.BlockSpec((1, tm, d), lambda b, i: (b, i, 0)),
            y_spec, y_spec, y_spec, y_spec,
            _mod_spec(layer, 2),
            pl.BlockSpec((1, d), const),
            _mod_spec(layer, 3),
            _mod_spec(layer, 4),
            _mod_spec(layer, 5),
            pl.BlockSpec((d, d), const, pipeline_mode=pl.Buffered(1)),
            pl.BlockSpec((d, D_FF), const, pipeline_mode=pl.Buffered(1)),
            pl.BlockSpec((D_FF, d), const, pipeline_mode=pl.Buffered(1)),
        ],
        out_specs=pl.BlockSpec((1, tm, d), lambda b, i: (b, i, 0)),
        out_shape=jax.ShapeDtypeStruct((batch, seq, d), F32),
        compiler_params=_params(("parallel", "parallel")),
        name="out_projection_mlp",
    )(x, *ys, mod5, gain.reshape(1, d), mod5, mod5, mod5, w_out, w1, w2)


def _log_sigmoid(t):
    return jnp.minimum(t, 0.0) - jnp.log1p(jnp.exp(-jnp.abs(t)))


def _segment_rms(x, gain, ones_bd, seg):
    ss = _segment_sum(x * x, ones_bd)
    return x * lax.rsqrt(ss * (1.0 / seg) + NORM_EPS) * gain


def _split3_f32(x):
    hi = x.astype(BF16).astype(F32)
    r1 = x - hi
    mid = r1.astype(BF16).astype(F32)
    lo = (r1 - mid).astype(BF16).astype(F32)
    return hi, mid, lo


def _prep_kernel(slopes_ref, dq_ref, dk_ref, dv_ref, fq_ref, fk_ref, fv_ref, zg_ref,
                 dqn_ref, dkn_ref, fqn_ref, fkn_ref, gbias_ref, bd32_ref, bd64_ref,
                 dqt_o, dk_o, dvt_o, fqt_o, fk_o, fvt_o, grow_o, carry_ref, *, tile):
    bd32 = bd32_ref[...]
    bd64 = bd64_ref[...]
    t0 = pl.program_id(1) * tile

    @pl.when(pl.program_id(1) == 0)
    def _():
        carry_ref[...] = jnp.zeros_like(carry_ref)

    t = zg_ref[0].T[0:GATE_ROWS, :] + gbias_ref[...]
    row = lax.broadcasted_iota(jnp.int32, (GATE_ROWS, tile), 0)
    lane = lax.broadcasted_iota(jnp.int32, (GATE_ROWS, tile), 1)
    whole_seq = row < ROW_ML_B
    per_chunk = (row >= ROW_ML_B) & (row < ROW_ML_I)
    acc = jnp.where(whole_seq | per_chunk, _log_sigmoid(t), 0.0)
    lane_in_chunk = lane & (CHUNK - 1)
    shift = 1
    while shift < tile:
        ok = whole_seq & (lane >= shift)
        if shift < CHUNK:
            ok = ok | (per_chunk & (lane_in_chunk >= shift))
        acc = acc + jnp.where(ok, pltpu.roll(acc, shift, 1), 0.0)
        shift *= 2
    acc = acc + jnp.where(whole_seq, carry_ref[:, 0:1], 0.0)
    carry_ref[...] = jnp.broadcast_to(acc[:, tile - 1:tile], carry_ref.shape)
    grow_o[0] = jnp.where(row < ROW_ML_I, acc, t)

    hd = HEAD_DIM
    lane_pos = (t0 + lax.broadcasted_iota(jnp.int32, (1, tile), 1)).astype(F32)
    row_pos = (t0 + lax.broadcasted_iota(jnp.int32, (tile, KEY_WIDTH), 0)).astype(F32)
    feat_row = lax.broadcasted_iota(jnp.int32, (hd, tile), 0)
    feat_lane = lax.broadcasted_iota(jnp.int32, (tile, KEY_WIDTH), 1) - hd

    def query_aug(bias_row):
        hi, mid, lo = _split3_f32(bias_row)
        return jnp.where(feat_row < 3, 1.0,
                         jnp.where(feat_row == 3, hi,
                                   jnp.where(feat_row == 4, mid, jnp.where(feat_row == 5, lo, 0.0))))

    def key_aug(k_all, h, bias_col):
        blk = k_all[:, (h // 2) * KEY_WIDTH:(h // 2 + 1) * KEY_WIDTH]
        k_head = pltpu.roll(blk, hd, 1) if h % 2 else blk
        hi, mid, lo = _split3_f32(bias_col)
        return jnp.where(feat_lane < 0, k_head,
                         jnp.where(feat_lane == 0, hi,
                                   jnp.where(feat_lane == 1, mid,
                                             jnp.where(feat_lane == 2, lo,
                                                       jnp.where(feat_lane < 6, 1.0, 0.0)))))

    dq_t = (_segment_rms(dq_ref[0], dqn_ref[...], bd32, DIFF_QK_DIM) * (DIFF_QK_DIM ** -0.5 * LOG2E)).T
    dk = _segment_rms(dk_ref[0], dkn_ref[...], bd32, DIFF_QK_DIM)
    dv_t = dv_ref[0].T
    fq_t = (_segment_rms(fq_ref[0], fqn_ref[...], bd64, hd) * (hd ** -0.5 * LOG2E)).T
    fk = _segment_rms(fk_ref[0], fkn_ref[...], bd64, hd)
    fv_t = fv_ref[0].T
    for h in range(HEADS):
        rows = slice(h * hd, (h + 1) * hd)
        slope = slopes_ref[h] * LOG2E
        aug = query_aug(-slope * lane_pos)
        for comp in range(2):
            in_comp = (feat_row >= comp * DIFF_QK_DIM) & (feat_row < (comp + 1) * DIFF_QK_DIM)
            q_comp = jnp.where(in_comp, dq_t[rows], 0.0)
            dqt_o[0, h, comp] = jnp.concatenate([q_comp, aug], axis=0).astype(BF16)
        dk_o[0, h] = key_aug(dk, h, slope * row_pos).astype(BF16)
        dvt_o[0, h] = dv_t[rows].astype(BF16)
        f_row = acc[ROW_FOX_F + h:ROW_FOX_F + h + 1, :] * LOG2E
        fqt_o[0, h] = jnp.concatenate([fq_t[rows], query_aug(f_row)], axis=0).astype(BF16)
        f_col = jnp.broadcast_to(f_row, (KEY_WIDTH, tile)).T
        fk_o[0, h] = key_aug(fk, h, -f_col).astype(BF16)
        fvt_o[0, h] = fv_t[rows].astype(BF16)


def _attention_prep(z, slopes, layer_params, tile):
    batch, seq, _ = z.shape
    g = GROUP_WIDTH
    dqn, dkn, fqn, fkn, gbias, bd32, bd64 = layer_params

    def zspec(block):
        return pl.BlockSpec((1, tile, g), lambda b, i: (b, i, block))

    const = lambda b, i: (0, 0)
    qt_spec = pl.BlockSpec((1, HEADS, KEY_WIDTH, tile), lambda b, i: (b, 0, 0, i))
    qt2_spec = pl.BlockSpec((1, HEADS, 2, KEY_WIDTH, tile), lambda b, i: (b, 0, 0, 0, i))
    k_spec = pl.BlockSpec((1, HEADS, tile, KEY_WIDTH), lambda b, i: (b, 0, i, 0))
    vt_spec = pl.BlockSpec((1, HEADS, HEAD_DIM, tile), lambda b, i: (b, 0, 0, i))
    qt_shape = jax.ShapeDtypeStruct((batch, HEADS, KEY_WIDTH, seq), BF16)
    qt2_shape = jax.ShapeDtypeStruct((batch, HEADS, 2, KEY_WIDTH, seq), BF16)
    k_shape = jax.ShapeDtypeStruct((batch, HEADS, seq, KEY_WIDTH), BF16)
    vt_shape = jax.ShapeDtypeStruct((batch, HEADS, HEAD_DIM, seq), BF16)
    return pl.pallas_call(
        functools.partial(_prep_kernel, tile=tile),
        grid=(batch, seq // tile),
        in_specs=[
            pl.BlockSpec(memory_space=pltpu.SMEM),
            zspec(Z_DQ), zspec(Z_DK), zspec(Z_DV), zspec(Z_FQ), zspec(Z_FK), zspec(Z_FV),
            pl.BlockSpec((1, tile, GATE_LANES), lambda b, i: (b, i, Z_GATE_BLOCK)),
            pl.BlockSpec((1, g), const), pl.BlockSpec((1, g), const),
            pl.BlockSpec((1, g), const), pl.BlockSpec((1, g), const),
            pl.BlockSpec((GATE_ROWS, 1), const),
            pl.BlockSpec((g, g), const), pl.BlockSpec((g, g), const),
        ],
        out_specs=[qt2_spec, k_spec, vt_spec, qt_spec, k_spec, vt_spec,
                   pl.BlockSpec((1, GATE_ROWS, tile), lambda b, i: (b, 0, i))],
        out_shape=[qt2_shape, k_shape, vt_shape, qt_shape, k_shape, vt_shape,
                   jax.ShapeDtypeStruct((batch, GATE_ROWS, seq), F32)],
        scratch_shapes=[pltpu.VMEM((GATE_ROWS, GATE_LANES), F32)],
        compiler_params=_params(("parallel", "arbitrary")),
        name="attention_prep",
    )(slopes, z, z, z, z, z, z, z, dqn, dkn, fqn, fkn, gbias, bd32, bd64)


def _attention_sweep(qt_of_head, k_ref, vt_ref, qi, tq, tk, ncomp, scratch):
    cols = ncomp * tq
    assert tq == 2 * tk
    s_scr, p_scr, m_scr, l_scr, a_scr, acc_scr = scratch
    m_scr[...] = jnp.full(m_scr.shape, -jnp.inf, F32)
    l_scr[...] = jnp.zeros(l_scr.shape, F32)
    a_scr[...] = jnp.ones(a_scr.shape, F32)
    acc_scr[...] = jnp.zeros(acc_scr.shape, F32)
    p_scr[1] = jnp.zeros(p_scr.shape[1:], BF16)
    key_minus_query = (lax.broadcasted_iota(jnp.int32, (tk, cols), 0)
                       - (lax.broadcasted_iota(jnp.int32, (tk, cols), 1) & (tq - 1)))

    def logits(h, j):
        start = pl.multiple_of(j * tk, tk)
        return jnp.dot(k_ref[0, h, pl.ds(start, tk), :], qt_of_head(h), preferred_element_type=F32)

    def weighted_values(h, j, slot):
        start = pl.multiple_of(j * tk, tk)
        return jnp.dot(vt_ref[0, h, :, pl.ds(start, tk)], p_scr[slot, h], preferred_element_type=F32)

    def accumulate(pv):
        for h in range(HEADS):
            acc_scr[h] = a_scr[h] * acc_scr[h] + pv[h]

    def stage(j, slot, masked, prefetch, prev_j):
        pv = [weighted_values(h, prev_j, 1 - slot) for h in range(HEADS)]
        s_next = [logits(h, j + 1) for h in range(HEADS)] if prefetch else None
        alphas = []
        for h in range(HEADS):
            s = s_scr[slot, h]
            if masked:
                s = jnp.where(key_minus_query <= qi * tq - j * tk, s, -jnp.inf)
            m_prev = m_scr[h]
            m_new = jnp.maximum(m_prev, jnp.max(s, axis=0, keepdims=True))
            alphas.append(jnp.exp2(m_prev - m_new))
            p = jnp.exp2(s - m_new)
            l_scr[h] = alphas[h] * l_scr[h] + jnp.sum(p, axis=0, keepdims=True)
            p_scr[slot, h] = p.astype(BF16)
            m_scr[h] = m_new
        accumulate(pv)
        for h in range(HEADS):
            a_scr[h] = alphas[h]
            if prefetch:
                s_scr[1 - slot, h] = s_next[h]

    for h in range(HEADS):
        s_scr[0, h] = logits(h, 0)
    n_full = 2 * qi

    def tile_pair(i, carry):
        stage(2 * i, 0, False, True, jnp.maximum(2 * i - 1, 0))
        stage(2 * i + 1, 1, False, True, 2 * i)
        return carry

    lax.fori_loop(0, qi, tile_pair, 0)
    stage(n_full, 0, True, True, jnp.maximum(n_full - 1, 0))
    stage(n_full + 1, 1, True, False, n_full)
    accumulate([weighted_values(h, n_full + 1, 1) for h in range(HEADS)])


def _diff_attn_kernel(lam_ref, qt_ref, k_ref, vt_ref, gain_ref, bd64_ref, o_ref, ot_scr, *scratch,
                      tq, tk, out_scale):
    qi = pl.program_id(1)
    lam = lam_ref[0]

    def qt_of_head(h):
        return jnp.concatenate([qt_ref[0, h, 0], qt_ref[0, h, 1]], axis=1)

    _attention_sweep(qt_of_head, k_ref, vt_ref, qi, tq, tk, 2, scratch)
    l_scr, acc_scr = scratch[3], scratch[5]
    for h in range(HEADS):
        o = acc_scr[h] / l_scr[h]
        ot_scr[h * HEAD_DIM:(h + 1) * HEAD_DIM, :] = o[:, 0:tq] - lam * o[:, tq:2 * tq]
    y = _segment_rms(ot_scr[...].T, gain_ref[...] * out_scale, bd64_ref[...], HEAD_DIM)
    o_ref[0] = y.astype(BF16)


def _fox_attn_kernel(qt_ref, k_ref, vt_ref, g_ref, o_ref, ot_scr, *scratch, tq, tk):
    qi = pl.program_id(1)
    _attention_sweep(lambda h: qt_ref[0, h], k_ref, vt_ref, qi, tq, tk, 1, scratch)
    l_scr, acc_scr = scratch[3], scratch[5]
    for h in range(HEADS):
        ot_scr[h * HEAD_DIM:(h + 1) * HEAD_DIM, :] = acc_scr[h] / l_scr[h]
    o_ref[0] = (ot_scr[...].T * jax.nn.sigmoid(g_ref[0])).astype(BF16)


def _attn_scratch(tq, tk, ncomp):
    cols = ncomp * tq
    return [
        pltpu.VMEM((GROUP_WIDTH, tq), F32),
        pltpu.VMEM((2, HEADS, tk, cols), F32),
        pltpu.VMEM((2, HEADS, tk, cols), BF16),
        pltpu.VMEM((HEADS, 1, cols), F32),
        pltpu.VMEM((HEADS, 1, cols), F32),
        pltpu.VMEM((HEADS, 1, cols), F32),
        pltpu.VMEM((HEADS, HEAD_DIM, cols), F32),
    ]


def _diff_attention(qt, k, vt, lam, gain, bd64, out_scale, tq, tk):
    batch, _, seq, _ = k.shape
    g = GROUP_WIDTH
    return pl.pallas_call(
        functools.partial(_diff_attn_kernel, tq=tq, tk=tk, out_scale=out_scale),
        grid=(batch, seq // tq),
        in_specs=[
            pl.BlockSpec(memory_space=pltpu.SMEM),
            pl.BlockSpec((1, HEADS, 2, KEY_WIDTH, tq), lambda b, i: (b, 0, 0, 0, i)),
            pl.BlockSpec((1, HEADS, seq, KEY_WIDTH), lambda b, i: (b, 0, 0, 0)),
            pl.BlockSpec((1, HEADS, HEAD_DIM, seq), lambda b, i: (b, 0, 0, 0)),
            pl.BlockSpec((1, g), lambda b, i: (0, 0)),
            pl.BlockSpec((g, g), lambda b, i: (0, 0)),
        ],
        out_specs=pl.BlockSpec((1, tq, g), lambda b, i: (b, i, 0)),
        out_shape=jax.ShapeDtypeStruct((batch, seq, g), BF16),
        scratch_shapes=_attn_scratch(tq, tk, 2),
        compiler_params=_params(("parallel", "arbitrary")),
        name="diff_attention",
    )(lam, qt, k, vt, gain, bd64)


def _fox_attention(qt, k, vt, z, tq, tk):
    batch, _, seq, _ = k.shape
    g = GROUP_WIDTH
    return pl.pallas_call(
        functools.partial(_fox_attn_kernel, tq=tq, tk=tk),
        grid=(batch, seq // tq),
        in_specs=[
            pl.BlockSpec((1, HEADS, KEY_WIDTH, tq), lambda b, i: (b, 0, 0, i)),
            pl.BlockSpec((1, HEADS, seq, KEY_WIDTH), lambda b, i: (b, 0, 0, 0)),
            pl.BlockSpec((1, HEADS, HEAD_DIM, seq), lambda b, i: (b, 0, 0, 0)),
            pl.BlockSpec((1, tq, g), lambda b, i: (b, i, Z_FG)),
        ],
        out_specs=pl.BlockSpec((1, tq, g), lambda b, i: (b, i, 0)),
        out_shape=jax.ShapeDtypeStruct((batch, seq, g), BF16),
        scratch_shapes=_attn_scratch(tq, tk, 1),
        compiler_params=_params(("parallel", "arbitrary")),
        name="fox_attention",
    )(qt, k, vt, z)


def _hgrn2_kernel(q_ref, f_ref, i_ref, g_ref, lb_ref, gain_ref, bd64_ref, tri_ref, mask_ref,
                  o_ref, st_ref, *, chunks):
    @pl.when(pl.program_id(1) == 0)
    def _():
        st_ref[...] = jnp.zeros_like(st_ref)

    lb = lb_ref[...]
    bd64 = bd64_ref[...]
    tri = tri_ref[...]
    sub = 8
    row8 = lax.broadcasted_iota(jnp.int32, (sub, GROUP_WIDTH), 0)

    def chunk(c, carry):
        rows = pl.ds(pl.multiple_of(c * CHUNK, CHUNK), CHUNK)
        q = q_ref[0, rows, :]
        f = lb + (1.0 - lb) * jax.nn.sigmoid(f_ref[0, rows, :])
        kk = 1.0 - f
        v = i_ref[0, rows, :]
        hi, mid, lo = _split3(jnp.log(f))
        b = (jnp.dot(tri, hi, preferred_element_type=F32)
             + jnp.dot(tri, mid, preferred_element_type=F32)
             + jnp.dot(tri, lo, preferred_element_type=F32))

        o_rows = [jnp.zeros((sub, GROUP_WIDTH), F32) for _ in range(CHUNK // sub)]
        for lag in range(CHUNK):
            r8 = lag // sub * sub
            if lag == 0:
                x = q * kk
                vd = v
            else:
                kd = pltpu.roll(kk, lag, 0)[r8:]
                bd = pltpu.roll(b, lag, 0)[r8:]
                vd = pltpu.roll(v, lag, 0)[r8:]
                x = q[r8:] * kd * jnp.exp(b[r8:] - bd)
                if lag % sub:
                    head = jnp.where(row8 >= lag - r8, x[0:sub], 0.0)
                    x = jnp.concatenate([head, x[sub:]], axis=0) if r8 + sub < CHUNK else head
            a = jnp.dot(x.astype(BF16), bd64, preferred_element_type=F32)
            contrib = a * vd
            for blk in range(r8 // sub, CHUNK // sub):
                lo_r = blk * sub - r8
                o_rows[blk] = o_rows[blk] + contrib[lo_r:lo_r + sub]
        o = jnp.concatenate(o_rows, axis=0)

        st = st_ref[...]
        qe = (q * jnp.exp(b)).astype(BF16)
        o = o + lax.dot_general(qe, st.astype(BF16), (((1,), (1,)), ((), ())),
                                preferred_element_type=F32)
        b_last = b[CHUNK - 1:CHUNK, :]
        kp = (kk * jnp.exp(b_last - b)).astype(BF16)
        upd = lax.dot_general(v.astype(BF16), kp, (((0,), (0,)), ((), ())),
                              preferred_element_type=F32)
        st_ref[...] = mask_ref[...] * (st * jnp.exp(b_last) + upd)

        y = _segment_rms(o, gain_ref[...], bd64, HEAD_DIM)
        g = g_ref[0, rows, :]
        o_ref[0, rows, :] = (y * (g * jax.nn.sigmoid(g))).astype(BF16)
        return carry

    lax.fori_loop(0, chunks, chunk, 0)


def _hgrn2(z, lb, gain, bd64, tri, mask, tile):
    batch, seq, _ = z.shape
    g = GROUP_WIDTH

    def zspec(block):
        return pl.BlockSpec((1, tile, g), lambda b, i: (b, i, block))

    const = lambda b, i: (0, 0)
    return pl.pallas_call(
        functools.partial(_hgrn2_kernel, chunks=tile // CHUNK),
        grid=(batch, seq // tile),
        in_specs=[
            zspec(Z_HQ), zspec(Z_HF), zspec(Z_HI), zspec(Z_HG),
            pl.BlockSpec((1, g), const), pl.BlockSpec((1, g), const),
            pl.BlockSpec((g, g), const), pl.BlockSpec((CHUNK, CHUNK), const),
            pl.BlockSpec((g, g), const),
        ],
        out_specs=pl.BlockSpec((1, tile, g), lambda b, i: (b, i, 0)),
        out_shape=jax.ShapeDtypeStruct((batch, seq, g), BF16),
        scratch_shapes=[pltpu.VMEM((g, g), F32)],
        compiler_params=_params(("parallel", "arbitrary")),
        name="hgrn2",
    )(z, z, z, z, lb, gain, bd64, tri, mask)


def _mlstm_kernel(q_ref, k_ref, v_ref, og_ref, gcol_ref, grow_ref, conv_ref, y_ref,
                  prev_scr, c_scr, n_scr, m_scr, y_scr, *, tile):
    @pl.when(pl.program_id(1) == 0)
    def _():
        prev_scr[...] = jnp.zeros_like(prev_scr)
        c_scr[...] = jnp.zeros_like(c_scr)
        n_scr[...] = jnp.zeros_like(n_scr)
        m_scr[...] = jnp.zeros_like(m_scr)

    g = GROUP_WIDTH
    halo = prev_scr.shape[0]
    raw = jnp.concatenate([q_ref[0], k_ref[0]], axis=1)
    padded = jnp.concatenate([prev_scr[...], raw], axis=0)
    w = conv_ref[...]
    conv = w[CONV_WIDTH - 1:CONV_WIDTH, :] * raw
    for back in range(1, CONV_WIDTH):
        tap = CONV_WIDTH - 1 - back
        conv = conv + w[tap:tap + 1, :] * pltpu.roll(padded, back, 0)[halo:]
    prev_scr[...] = raw[tile - halo:]
    act = conv * jax.nn.sigmoid(conv)
    qc = act[:, :g] * HEAD_DIM ** -0.5
    kc = act[:, g:]
    v = v_ref[0]

    r_idx = lax.broadcasted_iota(jnp.int32, (CHUNK, CHUNK), 0)
    c_idx = lax.broadcasted_iota(jnp.int32, (CHUNK, CHUNK), 1)
    causal = c_idx <= r_idx
    for c in range(tile // CHUNK):
        r0 = c * CHUNK
        for h in range(HEADS):
            hs = slice(h * HEAD_DIM, (h + 1) * HEAD_DIM)
            qh = qc[r0:r0 + CHUNK, hs]
            kh = kc[r0:r0 + CHUNK, hs]
            vh = v[r0:r0 + CHUNK, hs].astype(BF16)
            b_col = gcol_ref[0, r0:r0 + CHUNK, ROW_ML_B + h:ROW_ML_B + h + 1]
            i_col = gcol_ref[0, r0:r0 + CHUNK, ROW_ML_I + h:ROW_ML_I + h + 1]
            b_row = grow_ref[0, ROW_ML_B + h:ROW_ML_B + h + 1, r0:r0 + CHUNK]
            i_row = grow_ref[0, ROW_ML_I + h:ROW_ML_I + h + 1, r0:r0 + CHUNK]
            m = m_scr[h][0:1, 0:1]
            cm = c_scr[h]
            nv = n_scr[h]

            log_d = jnp.where(causal, b_col - b_row + i_row, -jnp.inf)
            inter = b_col + m
            m_t = jnp.maximum(inter, jnp.max(log_d, axis=-1, keepdims=True))
            dmat = jnp.exp(log_d - m_t)
            w_inter = jnp.exp(inter - m_t)
            qb = qh.astype(BF16)
            s = lax.dot_general(qb, kh.astype(BF16), (((1,), (1,)), ((), ())),
                                preferred_element_type=F32) * dmat
            num = (jnp.dot(s.astype(BF16), vh, preferred_element_type=F32)
                   + w_inter * jnp.dot(qb, cm.astype(BF16), preferred_element_type=F32))
            den = (jnp.sum(s, axis=-1, keepdims=True)
                   + w_inter * jnp.sum(qh * nv, axis=-1, keepdims=True))
            y_scr[r0:r0 + CHUNK, hs] = num / jnp.maximum(jnp.abs(den), jnp.exp(-m_t))

            b_last = b_col[CHUNK - 1:CHUNK, :]
            log_w = b_last - b_col + i_col
            m_new = jnp.maximum(b_last + m, jnp.max(log_w, axis=0, keepdims=True))
            decay = jnp.exp(b_last + m - m_new)
            kw = kh * jnp.exp(log_w - m_new)
            c_scr[h] = decay * cm + lax.dot_general(kw.astype(BF16), vh, (((0,), (0,)), ((), ())),
                                                    preferred_element_type=F32)
            n_scr[h] = decay * nv + jnp.sum(kw, axis=0, keepdims=True)
            m_scr[h] = jnp.broadcast_to(m_new, m_scr.shape[1:])

    y_ref[0] = (jax.nn.sigmoid(og_ref[0]) * y_scr[...]).astype(BF16)


def _mlstm(z, gcol, grow, conv_w, tile):
    batch, seq, _ = z.shape
    g = GROUP_WIDTH

    def zspec(block):
        return pl.BlockSpec((1, tile, g), lambda b, i: (b, i, block))

    return pl.pallas_call(
        functools.partial(_mlstm_kernel, tile=tile),
        grid=(batch, seq // tile),
        in_specs=[
            zspec(Z_MQ), zspec(Z_MK), zspec(Z_MV), zspec(Z_MO),
            pl.BlockSpec((1, tile, GATE_ROWS), lambda b, i: (b, i, 0)),
            pl.BlockSpec((1, GATE_ROWS, tile), lambda b, i: (b, 0, i)),
            pl.BlockSpec((CONV_WIDTH, 2 * g), lambda b, i: (0, 0)),
        ],
        out_specs=pl.BlockSpec((1, tile, g), lambda b, i: (b, i, 0)),
        out_shape=jax.ShapeDtypeStruct((batch, seq, g), BF16),
        scratch_shapes=[
            pltpu.VMEM((8, 2 * g), F32),
            pltpu.VMEM((HEADS, HEAD_DIM, HEAD_DIM), F32),
            pltpu.VMEM((HEADS, 1, HEAD_DIM), F32),
            pltpu.VMEM((HEADS, 8, 128), F32),
            pltpu.VMEM((tile, g), F32),
        ],
        compiler_params=_params(("parallel", "arbitrary")),
        name="mlstm",
    )(z, z, z, z, gcol, grow, conv_w)


def _tile_rows(v, reps):
    return jnp.tile(v.astype(F32), reps).reshape(1, -1)


def kernel(x, c, w_ada, b_ada, norm_mix_gain, norm_ff_gain, w_in, w_out, hg_lb_logits, hg_norm_gain,
           diff_qn_gain, diff_kn_gain, diff_lambda, diff_sub_gain, fox_qn_gain, fox_kn_gain, fox_f_bias,
           mlstm_conv, mlstm_i_bias, mlstm_f_bias, w_ff1, w_ff2):
    depth = w_in.shape[0]
    batch, seq, d = x.shape
    g, hd = GROUP_WIDTH, HEAD_DIM
    tm = min(512, seq)
    tq = min(256, seq)
    tk = 128
    tr = min(256, seq)

    p_lb = jax.nn.softmax(hg_lb_logits.astype(F32), axis=0)
    lower_bounds = jnp.cumsum(p_lb, axis=0) - p_lb[0:1]
    perm = _in_column_permutation()
    perm_idx = jnp.asarray(np.maximum(perm, 0), jnp.int32)
    perm_valid = jnp.asarray(perm >= 0)
    bd32 = _block_diag_ones(g, DIFF_QK_DIM)
    bd64 = _block_diag_ones(g, hd)
    bd64_f32 = bd64.astype(F32)
    tri = jnp.asarray(np.tril(np.ones((CHUNK, CHUNK), np.float32)), dtype=BF16)
    slopes = jnp.asarray(2.0 ** (-8.0 * np.arange(1, HEADS + 1) / HEADS), F32)

    mod = _ada_modulation(c.astype(F32), w_ada, b_ada)
    mod5 = mod.reshape(depth, batch, 6, 1, d)

    for l in range(depth):
        w_perm = jnp.where(perm_valid[None, :], jnp.take(w_in[l], perm_idx, axis=1), 0.0).astype(BF16)
        z = _input_projection(x, mod5, l, norm_mix_gain[l], w_perm, tm)

        gbias = jnp.zeros((GATE_ROWS,), F32)
        gbias = gbias.at[ROW_FOX_F:ROW_FOX_F + HEADS].set(fox_f_bias[l].astype(F32))
        gbias = gbias.at[ROW_ML_B:ROW_ML_B + HEADS].set(mlstm_f_bias[l].astype(F32))
        gbias = gbias.at[ROW_ML_I:ROW_ML_I + HEADS].set(mlstm_i_bias[l].astype(F32))
        prep_params = (
            _tile_rows(diff_qn_gain[l], 2 * HEADS), _tile_rows(diff_kn_gain[l], 2 * HEADS),
            _tile_rows(fox_qn_gain[l], HEADS), _tile_rows(fox_kn_gain[l], HEADS),
            gbias.reshape(GATE_ROWS, 1), bd32, bd64,
        )
        dqt, dk, dvt, fqt, fk, fvt, grow = _attention_prep(z, slopes, prep_params, tm)
        gcol = jnp.transpose(grow, (0, 2, 1))

        lam_init = 0.8 - 0.6 * math.exp(-0.3 * l)
        lv = diff_lambda[l].astype(F32)
        lam = jnp.exp(jnp.sum(lv[0] * lv[1])) - jnp.exp(jnp.sum(lv[2] * lv[3])) + lam_init

        y_a = _hgrn2(z, lower_bounds[l].reshape(1, g), _tile_rows(hg_norm_gain[l], HEADS),
                     bd64, tri, bd64_f32, tr)
        y_b = _diff_attention(dqt, dk, dvt, lam.reshape(1), _tile_rows(diff_sub_gain[l], HEADS),
                              bd64, 1.0 - lam_init, tq, tk)
        y_c = _fox_attention(fqt, fk, fvt, z, tq, tk)
        y_d = _mlstm(z, gcol, grow, mlstm_conv[l].astype(F32), tr)

        x = _out_projection_mlp(x, (y_a, y_b, y_c, y_d), mod5, l, norm_ff_gain[l],
                                w_out[l].astype(BF16), w_ff1[l].astype(BF16), w_ff2[l].astype(BF16), tm)
    return x
```

```python
import functools
import math

import numpy as np
import jax
import jax.numpy as jnp
from jax import lax
from jax.experimental import pallas as pl
from jax.experimental.pallas import tpu as pltpu

F32 = jnp.float32
BF16 = jnp.bfloat16

D_MODEL = 1024
N_GROUPS = 4
GROUP_WIDTH = D_MODEL // N_GROUPS
HEADS = 4
HEAD_DIM = GROUP_WIDTH // HEADS
DIFF_QK_DIM = HEAD_DIM // 2
D_FF = 4 * D_MODEL
CHUNK = 64
CONV_WIDTH = 4
NORM_EPS = 1e-6
N_FULL_GROUPS = 15
GATE_LANES = 128
Z_COLS = N_FULL_GROUPS * GROUP_WIDTH + GATE_LANES
GATE_ROWS = 16
KEY_WIDTH = 128
LOG2E = math.log2(math.e)

(Z_HQ, Z_HF, Z_HI, Z_HG, Z_DQ, Z_DK, Z_DV, Z_FQ, Z_FK, Z_FV, Z_FG,
 Z_MQ, Z_MK, Z_MV, Z_MO) = range(N_FULL_GROUPS)
Z_GATE_BLOCK = N_FULL_GROUPS * GROUP_WIDTH // GATE_LANES
ROW_FOX_F, ROW_ML_B, ROW_ML_G, ROW_ML_CG = 0, 4, 8, 12
ROW_ML_I = ROW_ML_G
COL_ML_M, COL_ML_W_STATE, COL_ML_FLOOR, COL_ML_W_KEY = 0, 4, 8, 12

VMEM_LIMIT = 56 * 1024 * 1024


def _params(sem):
    return pltpu.CompilerParams(dimension_semantics=sem, vmem_limit_bytes=VMEM_LIMIT)


def _in_column_permutation():
    g, h = GROUP_WIDTH, HEADS
    widths = (g,) * 4 + (g,) * 3 + (g,) * 4 + (h,) + (g,) * 4 + (h,) * 2
    starts = np.concatenate([[0], np.cumsum(widths)[:-1]])
    (hq, hf, hi, hg, dq, dk, dv, fq, fk, fv, fg, ff, mq, mk, mv, mo, mi, mf) = range(18)
    order_full = [hq, hf, hi, hg, dq, dk, dv, fq, fk, fv, fg, mq, mk, mv, mo]
    cols = [np.arange(starts[i], starts[i] + widths[i]) for i in order_full]
    gates = np.full((GATE_LANES,), -1, np.int64)
    gates[ROW_FOX_F:ROW_FOX_F + h] = np.arange(starts[ff], starts[ff] + h)
    gates[ROW_ML_B:ROW_ML_B + h] = np.arange(starts[mf], starts[mf] + h)
    gates[ROW_ML_I:ROW_ML_I + h] = np.arange(starts[mi], starts[mi] + h)
    return np.concatenate(cols + [gates])


def _split3(x):
    hi = x.astype(BF16)
    r1 = x - hi.astype(F32)
    mid = r1.astype(BF16)
    lo = (r1 - mid.astype(F32)).astype(BF16)
    return hi, mid, lo


def _split2(x):
    hi = x.astype(BF16)
    lo = (x - hi.astype(F32)).astype(BF16)
    return hi, lo


def _segment_sum(x, ones_bd):
    hi, lo = _split2(x)
    return (jnp.dot(hi, ones_bd, preferred_element_type=F32)
            + jnp.dot(lo, ones_bd, preferred_element_type=F32))


def _block_diag_ones(n, seg):
    idx = np.arange(n) // seg
    return jnp.asarray((idx[:, None] == idx[None, :]).astype(np.float32), dtype=BF16)


def _ada_kernel(c_ref, w_ref, b_ref, o_ref):
    c = c_ref[...]
    c_act = c * jax.nn.sigmoid(c)
    o_ref[0] = jnp.dot(c_act, w_ref[0], preferred_element_type=F32) + b_ref[0]


def _ada_modulation(c, w_ada, b_ada):
    depth, d, n = w_ada.shape
    batch = c.shape[0]
    bn = 1024
    return pl.pallas_call(
        _ada_kernel,
        grid=(depth, n // bn),
        in_specs=[
            pl.BlockSpec((batch, d), lambda l, j: (0, 0)),
            pl.BlockSpec((1, d, bn), lambda l, j: (l, 0, j)),
            pl.BlockSpec((1, 1, bn), lambda l, j: (l, 0, j)),
        ],
        out_specs=pl.BlockSpec((1, batch, bn), lambda l, j: (l, 0, j)),
        out_shape=jax.ShapeDtypeStruct((depth, batch, n), F32),
        compiler_params=_params(("parallel", "parallel")),
        name="ada_modulation",
    )(c, w_ada, b_ada.reshape(depth, 1, n))


def _modulated_norm(x, gain, shift, scale):
    ms = jnp.mean(x * x, axis=-1, keepdims=True)
    return x * lax.rsqrt(ms + NORM_EPS) * gain * (1.0 + scale) + shift


def _inproj_kernel(x_ref, gain_ref, shift_ref, scale_ref, w_ref, z_ref, *, col_chunk):
    h = _modulated_norm(x_ref[0], gain_ref[...], shift_ref[...], scale_ref[...]).astype(BF16)
    for j in range(0, Z_COLS, col_chunk):
        n = min(col_chunk, Z_COLS - j)
        z_ref[0, :, j:j + n] = jnp.dot(h, w_ref[:, j:j + n], preferred_element_type=F32)


def _mod_spec(layer, which):
    return pl.BlockSpec((None, None, None, 1, D_MODEL), lambda b, i: (layer, b, which, 0, 0))


def _input_projection(x, mod5, layer, gain, w_perm, tm):
    batch, seq, d = x.shape
    return pl.pallas_call(
        functools.partial(_inproj_kernel, col_chunk=512),
        grid=(batch, seq // tm),
        in_specs=[
            pl.BlockSpec((1, tm, d), lambda b, i: (b, i, 0)),
            pl.BlockSpec((1, d), lambda b, i: (0, 0)),
            _mod_spec(layer, 0),
            _mod_spec(layer, 1),
            pl.BlockSpec((d, Z_COLS), lambda b, i: (0, 0)),
        ],
        out_specs=pl.BlockSpec((1, tm, Z_COLS), lambda b, i: (b, i, 0)),
        out_shape=jax.ShapeDtypeStruct((batch, seq, Z_COLS), F32),
        compiler_params=_params(("parallel", "parallel")),
        name="input_projection",
    )(x, gain.reshape(1, d), mod5, mod5, w_perm)


def _ffn_kernel(x_ref, ya_ref, yb_ref, yc_ref, yd_ref, gate1_ref, gain_ref, shift_ref, scale_ref,
                gate2_ref, wo_ref, w1_ref, w2_ref, o_ref, *, ff_chunk):
    g = GROUP_WIDTH
    y = jnp.dot(ya_ref[0], wo_ref[0:g, :], preferred_element_type=F32)
    y += jnp.dot(yb_ref[0], wo_ref[g:2 * g, :], preferred_element_type=F32)
    y += jnp.dot(yc_ref[0], wo_ref[2 * g:3 * g, :], preferred_element_type=F32)
    y += jnp.dot(yd_ref[0], wo_ref[3 * g:4 * g, :], preferred_element_type=F32)
    x1 = x_ref[0] + gate1_ref[...] * y
    h = _modulated_norm(x1, gain_ref[...], shift_ref[...], scale_ref[...]).astype(BF16)
    acc = jnp.zeros_like(x1)
    for j in range(0, D_FF, ff_chunk):
        u = jnp.dot(h, w1_ref[:, j:j + ff_chunk], preferred_element_type=F32)
        u = jnp.square(jnp.maximum(u, 0.0)).astype(BF16)
        acc += jnp.dot(u, w2_ref[j:j + ff_chunk, :], preferred_element_type=F32)
    o_ref[0] = x1 + gate2_ref[...] * acc


def _out_projection_mlp(x, ys, mod5, layer, gain, w_out, w1, w2, tm):
    batch, seq, d = x.shape
    g = GROUP_WIDTH
    y_spec = pl.BlockSpec((1, tm, g), lambda b, i: (b, i, 0))
    const = lambda b, i: (0, 0)
    return pl.pallas_call(
        functools.partial(_ffn_kernel, ff_chunk=512),
        grid=(batch, seq // tm),
        in_specs=[
            pl.BlockSpec((1, tm, d), lambda b, i: (b, i, 0)),
            y_spec, y_spec, y_spec, y_spec,
            _mod_spec(layer, 2),
            pl.BlockSpec((1, d), const),
            _mod_spec(layer, 3),
            _mod_spec(layer, 4),
            _mod_spec(layer, 5),
            pl.BlockSpec((d, d), const, pipeline_mode=pl.Buffered(1)),
            pl.BlockSpec((d, D_FF), const, pipeline_mode=pl.Buffered(1)),
            pl.BlockSpec((D_FF, d), const, pipeline_mode=pl.Buffered(1)),
        ],
        out_specs=pl.BlockSpec((1, tm, d), lambda b, i: (b, i, 0)),
        out_shape=jax.ShapeDtypeStruct((batch, seq, d), F32),
        compiler_params=_params(("parallel", "parallel")),
        name="out_projection_mlp",
    )(x, *ys, mod5, gain.reshape(1, d), mod5, mod5, mod5, w_out, w1, w2)


def _log_sigmoid(t):
    return jnp.minimum(t, 0.0) - jnp.log1p(jnp.exp(-jnp.abs(t)))


def _segment_rms(x, gain, ones_bd, seg):
    ss = _segment_sum(x * x, ones_bd)
    return x * lax.rsqrt(ss * (1.0 / seg) + NORM_EPS) * gain


def _split3_f32(x):
    hi = x.astype(BF16).astype(F32)
    r1 = x - hi
    mid = r1.astype(BF16).astype(F32)
    lo = (r1 - mid).astype(BF16).astype(F32)
    return hi, mid, lo


def _prep_kernel(slopes_ref, dq_ref, dk_ref, dv_ref, fq_ref, fk_ref, fv_ref, zg_ref,
                 dqn_ref, dkn_ref, fqn_ref, fkn_ref, gbias_ref, bd32_ref, bd64_ref,
                 dqt_o, dk_o, dvt_o, fqt_o, fk_o, fvt_o, grow_o, mcol_o, carry_ref, mcarry_ref, *, tile):
    bd32 = bd32_ref[...]
    bd64 = bd64_ref[...]
    t0 = pl.program_id(1) * tile

    @pl.when(pl.program_id(1) == 0)
    def _():
        carry_ref[...] = jnp.zeros_like(carry_ref)
        mcarry_ref[...] = jnp.zeros_like(mcarry_ref)

    t = zg_ref[0].T[0:GATE_ROWS, :] + gbias_ref[...]
    row = lax.broadcasted_iota(jnp.int32, (GATE_ROWS, tile), 0)
    lane = lax.broadcasted_iota(jnp.int32, (GATE_ROWS, tile), 1)
    whole_seq = row < ROW_ML_B
    per_chunk = (row >= ROW_ML_B) & (row < ROW_ML_I)
    acc = jnp.where(whole_seq | per_chunk, _log_sigmoid(t), 0.0)
    lane_in_chunk = lane & (CHUNK - 1)
    shift = 1
    while shift < tile:
        ok = whole_seq & (lane >= shift)
        if shift < CHUNK:
            ok = ok | (per_chunk & (lane_in_chunk >= shift))
        acc = acc + jnp.where(ok, pltpu.roll(acc, shift, 1), 0.0)
        shift *= 2
    acc = acc + jnp.where(whole_seq, carry_ref[:, 0:1], 0.0)
    carry_ref[...] = jnp.broadcast_to(acc[:, tile - 1:tile], carry_ref.shape)
    g_all = t - pltpu.roll(acc, ROW_ML_G - ROW_ML_B, 0)
    run_max = g_all
    shift = 1
    while shift < CHUNK:
        run_max = jnp.maximum(run_max, jnp.where(lane_in_chunk >= shift, pltpu.roll(run_max, shift, 1), -jnp.inf))
        shift *= 2
    grow_o[0] = jnp.where(row < ROW_ML_G, acc,
                          jnp.where(row < ROW_ML_CG, g_all, pltpu.roll(run_max, ROW_ML_CG - ROW_ML_G, 0)))

    def over_chunk(x, op):
        step = 1
        while step < CHUNK:
            x = op(x, jnp.where(lane_in_chunk < CHUNK - step, pltpu.roll(x, tile - step, 1), x))
            step *= 2
        return x

    b_g = pltpu.roll(acc, ROW_ML_G - ROW_ML_B, 0)
    b_last = over_chunk(b_g, jnp.minimum)
    g_max = over_chunk(run_max, jnp.maximum)
    a_map, d_map = b_last, g_max + b_last
    step = CHUNK
    while step < tile:
        earlier = lane >= step
        d_map = jnp.where(earlier, jnp.maximum(pltpu.roll(d_map, step, 1) + a_map, d_map), d_map)
        a_map = jnp.where(earlier, pltpu.roll(a_map, step, 1) + a_map, a_map)
        step *= 2
    m_tile = mcarry_ref[:, 0:1]
    m_after = jnp.maximum(m_tile + a_map, d_map)
    m_before = jnp.where(lane >= CHUNK, pltpu.roll(m_after, CHUNK, 1), m_tile)
    mcarry_ref[...] = jnp.broadcast_to(m_after[:, tile - 1:tile], mcarry_ref.shape)
    big_m = jnp.maximum(m_before, run_max)
    w_state = jnp.exp(m_before - big_m)
    floor = jnp.exp(-(b_g + big_m))
    w_key = jnp.exp(g_all - jnp.maximum(m_before, g_max))
    assert COL_ML_FLOOR == ROW_ML_G
    mcol_o[0] = jnp.where(
        row < COL_ML_W_STATE, pltpu.roll(big_m, GATE_ROWS + COL_ML_M - ROW_ML_G, 0),
        jnp.where(row < COL_ML_FLOOR, pltpu.roll(w_state, GATE_ROWS + COL_ML_W_STATE - ROW_ML_G, 0),
                  jnp.where(row < COL_ML_W_KEY, floor, pltpu.roll(w_key, COL_ML_W_KEY - ROW_ML_G, 0))))

    hd = HEAD_DIM
    lane_pos = (t0 + lax.broadcasted_iota(jnp.int32, (1, tile), 1)).astype(F32)
    row_pos = (t0 + lax.broadcasted_iota(jnp.int32, (tile, KEY_WIDTH), 0)).astype(F32)
    feat_row = lax.broadcasted_iota(jnp.int32, (hd, tile), 0)
    feat_lane = lax.broadcasted_iota(jnp.int32, (tile, KEY_WIDTH), 1) - hd

    def query_aug(bias_row):
        hi, mid, lo = _split3_f32(bias_row)
        return jnp.where(feat_row < 3, 1.0,
                         jnp.where(feat_row == 3, hi,
                                   jnp.where(feat_row == 4, mid, jnp.where(feat_row == 5, lo, 0.0))))

    def key_aug(k_all, h, bias_col):
        blk = k_all[:, (h // 2) * KEY_WIDTH:(h // 2 + 1) * KEY_WIDTH]
        k_head = pltpu.roll(blk, hd, 1) if h % 2 else blk
        hi, mid, lo = _split3_f32(bias_col)
        return jnp.where(feat_lane < 0, k_head,
                         jnp.where(feat_lane == 0, hi,
                                   jnp.where(feat_lane == 1, mid,
                                             jnp.where(feat_lane == 2, lo,
                                                       jnp.where(feat_lane < 6, 1.0, 0.0)))))

    dq_t = (_segment_rms(dq_ref[0], dqn_ref[...], bd32, DIFF_QK_DIM) * (DIFF_QK_DIM ** -0.5 * LOG2E)).T
    dk = _segment_rms(dk_ref[0], dkn_ref[...], bd32, DIFF_QK_DIM)
    dv_t = dv_ref[0].T
    fq_t = (_segment_rms(fq_ref[0], fqn_ref[...], bd64, hd) * (hd ** -0.5 * LOG2E)).T
    fk = _segment_rms(fk_ref[0], fkn_ref[...], bd64, hd)
    fv_t = fv_ref[0].T
    for h in range(HEADS):
        rows = slice(h * hd, (h + 1) * hd)
        slope = slopes_ref[h] * LOG2E
        aug = query_aug(-slope * lane_pos)
        for comp in range(2):
            in_comp = (feat_row >= comp * DIFF_QK_DIM) & (feat_row < (comp + 1) * DIFF_QK_DIM)
            q_comp = jnp.where(in_comp, dq_t[rows], 0.0)
            dqt_o[0, h, comp] = jnp.concatenate([q_comp, aug], axis=0).astype(BF16)
        dk_o[0, h] = key_aug(dk, h, slope * row_pos).astype(BF16)
        dvt_o[0, h] = dv_t[rows].astype(BF16)
        f_row = acc[ROW_FOX_F + h:ROW_FOX_F + h + 1, :] * LOG2E
        fqt_o[0, h] = jnp.concatenate([fq_t[rows], query_aug(f_row)], axis=0).astype(BF16)
        f_col = jnp.broadcast_to(f_row, (KEY_WIDTH, tile)).T
        fk_o[0, h] = key_aug(fk, h, -f_col).astype(BF16)
        fvt_o[0, h] = fv_t[rows].astype(BF16)


def _attention_prep(z, slopes, layer_params, tile):
    batch, seq, _ = z.shape
    g = GROUP_WIDTH
    dqn, dkn, fqn, fkn, gbias, bd32, bd64 = layer_params

    def zspec(block):
        return pl.BlockSpec((1, tile, g), lambda b, i: (b, i, block))

    const = lambda b, i: (0, 0)
    qt_spec = pl.BlockSpec((1, HEADS, KEY_WIDTH, tile), lambda b, i: (b, 0, 0, i))
    qt2_spec = pl.BlockSpec((1, HEADS, 2, KEY_WIDTH, tile), lambda b, i: (b, 0, 0, 0, i))
    k_spec = pl.BlockSpec((1, HEADS, tile, KEY_WIDTH), lambda b, i: (b, 0, i, 0))
    vt_spec = pl.BlockSpec((1, HEADS, HEAD_DIM, tile), lambda b, i: (b, 0, 0, i))
    qt_shape = jax.ShapeDtypeStruct((batch, HEADS, KEY_WIDTH, seq), BF16)
    qt2_shape = jax.ShapeDtypeStruct((batch, HEADS, 2, KEY_WIDTH, seq), BF16)
    k_shape = jax.ShapeDtypeStruct((batch, HEADS, seq, KEY_WIDTH), BF16)
    vt_shape = jax.ShapeDtypeStruct((batch, HEADS, HEAD_DIM, seq), BF16)
    return pl.pallas_call(
        functools.partial(_prep_kernel, tile=tile),
        grid=(batch, seq // tile),
        in_specs=[
            pl.BlockSpec(memory_space=pltpu.SMEM),
            zspec(Z_DQ), zspec(Z_DK), zspec(Z_DV), zspec(Z_FQ), zspec(Z_FK), zspec(Z_FV),
            pl.BlockSpec((1, tile, GATE_LANES), lambda b, i: (b, i, Z_GATE_BLOCK)),
            pl.BlockSpec((1, g), const), pl.BlockSpec((1, g), const),
            pl.BlockSpec((1, g), const), pl.BlockSpec((1, g), const),
            pl.BlockSpec((GATE_ROWS, 1), const),
            pl.BlockSpec((g, g), const), pl.BlockSpec((g, g), const),
        ],
        out_specs=[qt2_spec, k_spec, vt_spec, qt_spec, k_spec, vt_spec,
                   pl.BlockSpec((1, GATE_ROWS, tile), lambda b, i: (b, 0, i)),
                   pl.BlockSpec((1, GATE_ROWS, tile), lambda b, i: (b, 0, i))],
        out_shape=[qt2_shape, k_shape, vt_shape, qt_shape, k_shape, vt_shape,
                   jax.ShapeDtypeStruct((batch, GATE_ROWS, seq), F32),
                   jax.ShapeDtypeStruct((batch, GATE_ROWS, seq), F32)],
        scratch_shapes=[pltpu.VMEM((GATE_ROWS, GATE_LANES), F32), pltpu.VMEM((GATE_ROWS, GATE_LANES), F32)],
        compiler_params=_params(("parallel", "arbitrary")),
        name="attention_prep",
    )(slopes, z, z, z, z, z, z, z, dqn, dkn, fqn, fkn, gbias, bd32, bd64)


def _attention_sweep(qt_of_head, k_ref, vt_ref, qi, tq, tk, ncomp, scratch):
    cols = ncomp * tq
    assert tq == 2 * tk
    s_scr, p_scr, m_scr, l_scr, a_scr, acc_scr = scratch
    m_scr[...] = jnp.full(m_scr.shape, -jnp.inf, F32)
    l_scr[...] = jnp.zeros(l_scr.shape, F32)
    a_scr[...] = jnp.ones(a_scr.shape, F32)
    acc_scr[...] = jnp.zeros(acc_scr.shape, F32)
    p_scr[1] = jnp.zeros(p_scr.shape[1:], BF16)
    key_minus_query = (lax.broadcasted_iota(jnp.int32, (tk, cols), 0)
                       - (lax.broadcasted_iota(jnp.int32, (tk, cols), 1) & (tq - 1)))

    def logits(h, j):
        start = pl.multiple_of(j * tk, tk)
        return jnp.dot(k_ref[0, h, pl.ds(start, tk), :], qt_of_head(h), preferred_element_type=F32)

    def weighted_values(h, j, slot):
        start = pl.multiple_of(j * tk, tk)
        return jnp.dot(vt_ref[0, h, :, pl.ds(start, tk)], p_scr[slot, h], preferred_element_type=F32)

    def accumulate(pv):
        for h in range(HEADS):
            acc_scr[h] = a_scr[h] * acc_scr[h] + pv[h]

    def stage(j, slot, masked, prefetch, prev_j):
        pv = [weighted_values(h, prev_j, 1 - slot) for h in range(HEADS)]
        s_next = [logits(h, j + 1) for h in range(HEADS)] if prefetch else None
        alphas = []
        for h in range(HEADS):
            s = s_scr[slot, h]
            if masked:
                s = jnp.where(key_minus_query <= qi * tq - j * tk, s, -jnp.inf)
            m_prev = m_scr[h]
            m_new = jnp.maximum(m_prev, jnp.max(s, axis=0, keepdims=True))
            alphas.append(jnp.exp2(m_prev - m_new))
            p = jnp.exp2(s - m_new)
            l_scr[h] = alphas[h] * l_scr[h] + jnp.sum(p, axis=0, keepdims=True)
            p_scr[slot, h] = p.astype(BF16)
            m_scr[h] = m_new
        accumulate(pv)
        for h in range(HEADS):
            a_scr[h] = alphas[h]
            if prefetch:
                s_scr[1 - slot, h] = s_next[h]

    for h in range(HEADS):
        s_scr[0, h] = logits(h, 0)
    n_full = 2 * qi

    def tile_pair(i, carry):
        stage(2 * i, 0, False, True, jnp.maximum(2 * i - 1, 0))
        stage(2 * i + 1, 1, False, True, 2 * i)
        return carry

    lax.fori_loop(0, qi, tile_pair, 0)
    stage(n_full, 0, True, True, jnp.maximum(n_full - 1, 0))
    stage(n_full + 1, 1, True, False, n_full)
    accumulate([weighted_values(h, n_full + 1, 1) for h in range(HEADS)])


def _diff_attn_kernel(lam_ref, qt_ref, k_ref, vt_ref, gain_ref, bd64_ref, o_ref, ot_scr, *scratch,
                      tq, tk, out_scale):
    qi = pl.program_id(1)
    lam = lam_ref[0]

    def qt_of_head(h):
        return jnp.concatenate([qt_ref[0, h, 0], qt_ref[0, h, 1]], axis=1)

    _attention_sweep(qt_of_head, k_ref, vt_ref, qi, tq, tk, 2, scratch)
    l_scr, acc_scr = scratch[3], scratch[5]
    for h in range(HEADS):
        o = acc_scr[h] / l_scr[h]
        ot_scr[h * HEAD_DIM:(h + 1) * HEAD_DIM, :] = o[:, 0:tq] - lam * o[:, tq:2 * tq]
    y = _segment_rms(ot_scr[...].T, gain_ref[...] * out_scale, bd64_ref[...], HEAD_DIM)
    o_ref[0] = y.astype(BF16)


def _fox_attn_kernel(qt_ref, k_ref, vt_ref, g_ref, o_ref, ot_scr, *scratch, tq, tk):
    qi = pl.program_id(1)
    _attention_sweep(lambda h: qt_ref[0, h], k_ref, vt_ref, qi, tq, tk, 1, scratch)
    l_scr, acc_scr = scratch[3], scratch[5]
    for h in range(HEADS):
        ot_scr[h * HEAD_DIM:(h + 1) * HEAD_DIM, :] = acc_scr[h] / l_scr[h]
    o_ref[0] = (ot_scr[...].T * jax.nn.sigmoid(g_ref[0])).astype(BF16)


def _attn_scratch(tq, tk, ncomp):
    cols = ncomp * tq
    return [
        pltpu.VMEM((GROUP_WIDTH, tq), F32),
        pltpu.VMEM((2, HEADS, tk, cols), F32),
        pltpu.VMEM((2, HEADS, tk, cols), BF16),
        pltpu.VMEM((HEADS, 1, cols), F32),
        pltpu.VMEM((HEADS, 1, cols), F32),
        pltpu.VMEM((HEADS, 1, cols), F32),
        pltpu.VMEM((HEADS, HEAD_DIM, cols), F32),
    ]


def _diff_attention(qt, k, vt, lam, gain, bd64, out_scale, tq, tk):
    batch, _, seq, _ = k.shape
    g = GROUP_WIDTH
    return pl.pallas_call(
        functools.partial(_diff_attn_kernel, tq=tq, tk=tk, out_scale=out_scale),
        grid=(batch, seq // tq),
        in_specs=[
            pl.BlockSpec(memory_space=pltpu.SMEM),
            pl.BlockSpec((1, HEADS, 2, KEY_WIDTH, tq), lambda b, i: (b, 0, 0, 0, i)),
            pl.BlockSpec((1, HEADS, seq, KEY_WIDTH), lambda b, i: (b, 0, 0, 0)),
            pl.BlockSpec((1, HEADS, HEAD_DIM, seq), lambda b, i: (b, 0, 0, 0)),
            pl.BlockSpec((1, g), lambda b, i: (0, 0)),
            pl.BlockSpec((g, g), lambda b, i: (0, 0)),
        ],
        out_specs=pl.BlockSpec((1, tq, g), lambda b, i: (b, i, 0)),
        out_shape=jax.ShapeDtypeStruct((batch, seq, g), BF16),
        scratch_shapes=_attn_scratch(tq, tk, 2),
        compiler_params=_params(("parallel", "arbitrary")),
        name="diff_attention",
    )(lam, qt, k, vt, gain, bd64)


def _fox_attention(qt, k, vt, z, tq, tk):
    batch, _, seq, _ = k.shape
    g = GROUP_WIDTH
    return pl.pallas_call(
        functools.partial(_fox_attn_kernel, tq=tq, tk=tk),
        grid=(batch, seq // tq),
        in_specs=[
            pl.BlockSpec((1, HEADS, KEY_WIDTH, tq), lambda b, i: (b, 0, 0, i)),
            pl.BlockSpec((1, HEADS, seq, KEY_WIDTH), lambda b, i: (b, 0, 0, 0)),
            pl.BlockSpec((1, HEADS, HEAD_DIM, seq), lambda b, i: (b, 0, 0, 0)),
            pl.BlockSpec((1, tq, g), lambda b, i: (b, i, Z_FG)),
        ],
        out_specs=pl.BlockSpec((1, tq, g), lambda b, i: (b, i, 0)),
        out_shape=jax.ShapeDtypeStruct((batch, seq, g), BF16),
        scratch_shapes=_attn_scratch(tq, tk, 1),
        compiler_params=_params(("parallel", "arbitrary")),
        name="fox_attention",
    )(qt, k, vt, z)


def _hgrn2_kernel(q_ref, f_ref, i_ref, g_ref, lb_ref, gain_ref, bd64_ref, tri_ref, mask_ref,
                  o_ref, st_ref, *, chunks):
    @pl.when(pl.program_id(1) == 0)
    def _():
        st_ref[...] = jnp.zeros_like(st_ref)

    lb = lb_ref[...]
    bd64 = bd64_ref[...]
    tri = tri_ref[...]
    sub = 8
    gw = GROUP_WIDTH
    groups = CHUNK // sub
    sub_idx = lax.broadcasted_iota(jnp.int32, (groups, sub, gw), 1)
    row = lax.broadcasted_iota(jnp.int32, (CHUNK, gw), 0)
    lane_head = lax.broadcasted_iota(jnp.int32, (CHUNK, gw), 1) // HEAD_DIM
    key_pos = lax.broadcasted_iota(jnp.int32, (CHUNK, gw), 1) % HEAD_DIM

    def stack_heads(x):
        return jnp.concatenate([jnp.where(lane_head == h, x, 0.0) for h in range(HEADS)], axis=0).astype(BF16)

    span = range(chunks)
    rows = [slice(c * CHUNK, (c + 1) * CHUNK) for c in span]
    q = [q_ref[0, r, :] for r in rows]
    v = [i_ref[0, r, :] for r in rows]
    kk, b = [], []
    for c in span:
        f = lb + (1.0 - lb) * jax.nn.sigmoid(f_ref[0, rows[c], :])
        kk.append(1.0 - f)
        hi, mid, lo = _split3(jnp.log(f) * LOG2E)
        b.append(jnp.dot(tri, hi, preferred_element_type=F32)
                 + jnp.dot(tri, mid, preferred_element_type=F32)
                 + jnp.dot(tri, lo, preferred_element_type=F32))

    o = []
    for c in span:
        q3, k3, b3, v3 = (t.reshape(groups, sub, gw) for t in (q[c], kk[c], b[c], v[c]))
        o3 = jnp.zeros((groups, sub, gw), F32)
        for lag in range(sub):
            if lag == 0:
                x = q3 * k3
                vd = v3
            else:
                kd = pltpu.roll(k3, lag, 1)
                bd = pltpu.roll(b3, lag, 1)
                vd = pltpu.roll(v3, lag, 1)
                x = jnp.where(sub_idx >= lag, q3 * kd * jnp.exp2(b3 - bd), 0.0)
            a = jnp.dot(x.reshape(CHUNK, gw).astype(BF16), bd64, preferred_element_type=F32)
            o3 = o3 + a.reshape(groups, sub, gw) * vd
        o.append(o3.reshape(CHUNK, gw))

    a_far = []
    for c in span:
        total = None
        for half in (sub, 2 * sub, 4 * sub):
            block = 2 * half
            ref_rows = jnp.concatenate(
                [jnp.broadcast_to(b[c][p + half - 1:p + half, :], (block, gw)) for p in range(0, CHUNK, block)],
                axis=0)
            upper = ((row // half) & 1) == 1
            e = jnp.exp2(jnp.where(upper, b[c] - ref_rows, ref_rows - b[c]))
            qs = jnp.where(upper, q[c] * e, 0.0).astype(BF16)
            ks = stack_heads(jnp.where(upper, 0.0, kk[c] * e))
            a_lvl = lax.dot_general(qs, ks, (((1,), (1,)), ((), ())), preferred_element_type=F32)
            if block < CHUNK:
                a_lvl = jnp.where(row // block == key_pos // block, a_lvl, 0.0)
            total = a_lvl if total is None else total + a_lvl
        a_far.append(total.astype(BF16))
    for c in span:
        o[c] = o[c] + jnp.dot(a_far[c], stack_heads(v[c]), preferred_element_type=F32)

    upd = []
    for c in span:
        kp = (kk[c] * jnp.exp2(b[c][CHUNK - 1:CHUNK, :] - b[c])).astype(BF16)
        upd.append(lax.dot_general(v[c].astype(BF16), kp, (((0,), (0,)), ((), ())),
                                   preferred_element_type=F32))
    st = st_ref[...]
    for c in span:
        qe = (q[c] * jnp.exp2(b[c])).astype(BF16)
        o[c] = o[c] + lax.dot_general(qe, st.astype(BF16), (((1,), (1,)), ((), ())),
                                      preferred_element_type=F32)
        st = mask_ref[...] * (st * jnp.exp2(b[c][CHUNK - 1:CHUNK, :]) + upd[c])
    st_ref[...] = st

    for c in span:
        y = _segment_rms(o[c], gain_ref[...], bd64, HEAD_DIM)
        g = g_ref[0, rows[c], :]
        o_ref[0, rows[c], :] = (y * (g * jax.nn.sigmoid(g))).astype(BF16)


def _hgrn2(z, lb, gain, bd64, tri, mask, tile):
    batch, seq, _ = z.shape
    g = GROUP_WIDTH

    def zspec(block):
        return pl.BlockSpec((1, tile, g), lambda b, i: (b, i, block))

    const = lambda b, i: (0, 0)
    return pl.pallas_call(
        functools.partial(_hgrn2_kernel, chunks=tile // CHUNK),
        grid=(batch, seq // tile),
        in_specs=[
            zspec(Z_HQ), zspec(Z_HF), zspec(Z_HI), zspec(Z_HG),
            pl.BlockSpec((1, g), const), pl.BlockSpec((1, g), const),
            pl.BlockSpec((g, g), const), pl.BlockSpec((CHUNK, CHUNK), const),
            pl.BlockSpec((g, g), const),
        ],
        out_specs=pl.BlockSpec((1, tile, g), lambda b, i: (b, i, 0)),
        out_shape=jax.ShapeDtypeStruct((batch, seq, g), BF16),
        scratch_shapes=[pltpu.VMEM((g, g), F32)],
        compiler_params=_params(("parallel", "arbitrary")),
        name="hgrn2",
    )(z, z, z, z, lb, gain, bd64, tri, mask)


def _stack_heads(x):
    lane_head = lax.broadcasted_iota(jnp.int32, x.shape, 1) // HEAD_DIM
    return jnp.concatenate([jnp.where(lane_head == h, x, 0.0) for h in range(HEADS)], axis=0).astype(BF16)


def _mlstm_kernel(q_ref, k_ref, v_ref, og_ref, gcol_ref, grow_ref, conv_ref, expand_ref, bd64_ref,
                  mask_ref, y_ref, prev_scr, c_scr, n_scr, *, tile):
    @pl.when(pl.program_id(1) == 0)
    def _():
        prev_scr[...] = jnp.zeros_like(prev_scr)
        c_scr[...] = jnp.zeros_like(c_scr)
        n_scr[...] = jnp.zeros_like(n_scr)

    g = GROUP_WIDTH
    halo = prev_scr.shape[0]
    raw = jnp.concatenate([q_ref[0], k_ref[0]], axis=1)
    padded = jnp.concatenate([prev_scr[...], raw], axis=0)
    w = conv_ref[...]
    conv = w[CONV_WIDTH - 1:CONV_WIDTH, :] * raw
    for back in range(1, CONV_WIDTH):
        tap = CONV_WIDTH - 1 - back
        conv = conv + w[tap:tap + 1, :] * pltpu.roll(padded, back, 0)[halo:]
    prev_scr[...] = raw[tile - halo:]
    act = conv * jax.nn.sigmoid(conv)
    qc = act[:, :g] * HEAD_DIM ** -0.5
    kc = act[:, g:]
    v = v_ref[0]

    bd64 = bd64_ref[...]
    expand = expand_ref[...]
    chunks = tile // CHUNK
    span = range(chunks)
    rows = [slice(c * CHUNK, (c + 1) * CHUNK) for c in span]
    t_idx = lax.broadcasted_iota(jnp.int32, (CHUNK, g), 0)
    s_idx = lax.broadcasted_iota(jnp.int32, (CHUNK, g), 1) % HEAD_DIM
    causal = s_idx <= t_idx

    m_b, w_state_b, floor_b, w_key_b, g_b = [], [], [], [], []
    for c in span:
        hi, mid, lo = _split3(gcol_ref[0, rows[c], :])
        wide = (jnp.dot(hi, expand, preferred_element_type=F32)
                + jnp.dot(mid, expand, preferred_element_type=F32)
                + jnp.dot(lo, expand, preferred_element_type=F32))
        for dst, col in ((m_b, COL_ML_M), (w_state_b, COL_ML_W_STATE), (floor_b, COL_ML_FLOOR),
                         (w_key_b, COL_ML_W_KEY)):
            dst.append(wide[:, col // HEADS * g:(col // HEADS + 1) * g])
        g_b.append(jnp.concatenate(
            [grow_ref[0, ROW_ML_G + h:ROW_ML_G + h + 1, rows[c]] for h in range(HEADS)], axis=1))

    qb = [qc[r].astype(BF16) for r in rows]
    v_stack = [_stack_heads(v[r]) for r in rows]
    scores = [lax.dot_general(qb[c], _stack_heads(kc[rows[c]]), (((1,), (1,)), ((), ())),
                              preferred_element_type=F32) for c in span]
    num_intra, den_intra = [], []
    for c in span:
        s = scores[c] * jnp.exp(jnp.where(causal, g_b[c] - m_b[c], -jnp.inf))
        num_intra.append(jnp.dot(s.astype(BF16), v_stack[c], preferred_element_type=F32))
        den_intra.append(_segment_sum(s, bd64))
    c_upd, n_upd = [], []
    for c in span:
        kw = kc[rows[c]] * w_key_b[c]
        c_upd.append(mask_ref[...] * lax.dot_general(kw.astype(BF16), v[rows[c]].astype(BF16),
                                                     (((0,), (0,)), ((), ())), preferred_element_type=F32))
        n_upd.append(jnp.sum(kw, axis=0, keepdims=True))

    cm = c_scr[...]
    nv = n_scr[...]
    for c in span:
        inter = jnp.dot(qb[c], cm.astype(BF16), preferred_element_type=F32)
        q_dot_n = _segment_sum(qc[rows[c]] * nv, bd64)
        num = num_intra[c] + w_state_b[c] * inter
        den = den_intra[c] + w_state_b[c] * q_dot_n
        y = num / jnp.maximum(jnp.abs(den), floor_b[c])
        y_ref[0, rows[c], :] = (jax.nn.sigmoid(og_ref[0, rows[c], :]) * y).astype(BF16)
        decay = w_state_b[c][CHUNK - 1:CHUNK, :]
        cm = decay * cm + c_upd[c]
        nv = decay * nv + n_upd[c]
    c_scr[...] = cm
    n_scr[...] = nv


def _mlstm(z, gcol, grow, conv_w, bd64, mask, tile):
    batch, seq, _ = z.shape
    g = GROUP_WIDTH
    expand = jnp.asarray(np.repeat(np.eye(GATE_ROWS, dtype=np.float32), HEAD_DIM, axis=1), dtype=BF16)

    def zspec(block):
        return pl.BlockSpec((1, tile, g), lambda b, i: (b, i, block))

    const = lambda b, i: (0, 0)
    return pl.pallas_call(
        functools.partial(_mlstm_kernel, tile=tile),
        grid=(batch, seq // tile),
        in_specs=[
            zspec(Z_MQ), zspec(Z_MK), zspec(Z_MV), zspec(Z_MO),
            pl.BlockSpec((1, tile, GATE_ROWS), lambda b, i: (b, i, 0)),
            pl.BlockSpec((1, GATE_ROWS, tile), lambda b, i: (b, 0, i)),
            pl.BlockSpec((CONV_WIDTH, 2 * g), const),
            pl.BlockSpec((GATE_ROWS, GATE_ROWS * HEAD_DIM), const),
            pl.BlockSpec((g, g), const),
            pl.BlockSpec((g, g), const),
        ],
        out_specs=pl.BlockSpec((1, tile, g), lambda b, i: (b, i, 0)),
        out_shape=jax.ShapeDtypeStruct((batch, seq, g), BF16),
        scratch_shapes=[
            pltpu.VMEM((8, 2 * g), F32),
            pltpu.VMEM((g, g), F32),
            pltpu.VMEM((1, g), F32),
        ],
        compiler_params=_params(("parallel", "arbitrary")),
        name="mlstm",
    )(z, z, z, z, gcol, grow, conv_w, expand, bd64, mask)


def _tile_rows(v, reps):
    return jnp.tile(v.astype(F32), reps).reshape(1, -1)


def kernel(x, c, w_ada, b_ada, norm_mix_gain, norm_ff_gain, w_in, w_out, hg_lb_logits, hg_norm_gain,
           diff_qn_gain, diff_kn_gain, diff_lambda, diff_sub_gain, fox_qn_gain, fox_kn_gain, fox_f_bias,
           mlstm_conv, mlstm_i_bias, mlstm_f_bias, w_ff1, w_ff2):
    depth = w_in.shape[0]
    batch, seq, d = x.shape
    g, hd = GROUP_WIDTH, HEAD_DIM
    tm = min(512, seq)
    tq = min(256, seq)
    tk = 128
    tr = min(256, seq)

    p_lb = jax.nn.softmax(hg_lb_logits.astype(F32), axis=0)
    lower_bounds = jnp.cumsum(p_lb, axis=0) - p_lb[0:1]
    perm = _in_column_permutation()
    perm_idx = jnp.asarray(np.maximum(perm, 0), jnp.int32)
    perm_valid = jnp.asarray(perm >= 0)
    bd32 = _block_diag_ones(g, DIFF_QK_DIM)
    bd64 = _block_diag_ones(g, hd)
    bd64_f32 = bd64.astype(F32)
    tri = jnp.asarray(np.tril(np.ones((CHUNK, CHUNK), np.float32)), dtype=BF16)
    slopes = jnp.asarray(2.0 ** (-8.0 * np.arange(1, HEADS + 1) / HEADS), F32)

    mod = _ada_modulation(c.astype(F32), w_ada, b_ada)
    mod5 = mod.reshape(depth, batch, 6, 1, d)

    for l in range(depth):
        w_perm = jnp.where(perm_valid[None, :], jnp.take(w_in[l], perm_idx, axis=1), 0.0).astype(BF16)
        z = _input_projection(x, mod5, l, norm_mix_gain[l], w_perm, tm)

        gbias = jnp.zeros((GATE_ROWS,), F32)
        gbias = gbias.at[ROW_FOX_F:ROW_FOX_F + HEADS].set(fox_f_bias[l].astype(F32))
        gbias = gbias.at[ROW_ML_B:ROW_ML_B + HEADS].set(mlstm_f_bias[l].astype(F32))
        gbias = gbias.at[ROW_ML_I:ROW_ML_I + HEADS].set(mlstm_i_bias[l].astype(F32))
        prep_params = (
            _tile_rows(diff_qn_gain[l], 2 * HEADS), _tile_rows(diff_kn_gain[l], 2 * HEADS),
            _tile_rows(fox_qn_gain[l], HEADS), _tile_rows(fox_kn_gain[l], HEADS),
            gbias.reshape(GATE_ROWS, 1), bd32, bd64,
        )
        dqt, dk, dvt, fqt, fk, fvt, grow, mcol_t = _attention_prep(z, slopes, prep_params, tm)
        gcol = jnp.transpose(mcol_t, (0, 2, 1))

        lam_init = 0.8 - 0.6 * math.exp(-0.3 * l)
        lv = diff_lambda[l].astype(F32)
        lam = jnp.exp(jnp.sum(lv[0] * lv[1])) - jnp.exp(jnp.sum(lv[2] * lv[3])) + lam_init

        y_a = _hgrn2(z, lower_bounds[l].reshape(1, g), _tile_rows(hg_norm_gain[l], HEADS),
                     bd64, tri, bd64_f32, tr)
        y_b = _diff_attention(dqt, dk, dvt, lam.reshape(1), _tile_rows(diff_sub_gain[l], HEADS),
                              bd64, 1.0 - lam_init, tq, tk)
        y_c = _fox_attention(fqt, fk, fvt, z, tq, tk)
        y_d = _mlstm(z, gcol, grow, mlstm_conv[l].astype(F32), bd64, bd64_f32, tr)

        x = _out_projection_mlp(x, (y_a, y_b, y_c, y_d), mod5, l, norm_ff_gain[l],
                                w_out[l].astype(BF16), w_ff1[l].astype(BF16), w_ff2[l].astype(BF16), tm)
    return x
```

```python
import functools
import math

import numpy as np
import jax
import jax.numpy as jnp
from jax import lax
from jax.experimental import pallas as pl
from jax.experimental.pallas import tpu as pltpu

F32 = jnp.float32
BF16 = jnp.bfloat16

D_MODEL = 1024
N_GROUPS = 4
GROUP_WIDTH = D_MODEL // N_GROUPS
HEADS = 4
HEAD_DIM = GROUP_WIDTH // HEADS
DIFF_QK_DIM = HEAD_DIM // 2
D_FF = 4 * D_MODEL
CHUNK = 64
CONV_WIDTH = 4
NORM_EPS = 1e-6
N_FULL_GROUPS = 15
GATE_LANES = 128
Z_COLS = N_FULL_GROUPS * GROUP_WIDTH + GATE_LANES
GATE_ROWS = 16
KEY_WIDTH = 128
VT_ROWS = HEAD_DIM + 16
LOG2E = math.log2(math.e)

(Z_HQ, Z_HF, Z_HI, Z_HG, Z_DQ, Z_DK, Z_DV, Z_FQ, Z_FK, Z_FV, Z_FG,
 Z_MQ, Z_MK, Z_MV, Z_MO) = range(N_FULL_GROUPS)
Z_GATE_BLOCK = N_FULL_GROUPS * GROUP_WIDTH // GATE_LANES
ROW_FOX_F, ROW_ML_B, ROW_ML_G, ROW_ML_CG = 0, 4, 8, 12
ROW_ML_I = ROW_ML_G
COL_ML_M, COL_ML_W_STATE, COL_ML_FLOOR, COL_ML_W_KEY = 0, 4, 8, 12

VMEM_LIMIT = 56 * 1024 * 1024


def _params(sem):
    return pltpu.CompilerParams(dimension_semantics=sem, vmem_limit_bytes=VMEM_LIMIT)


def _in_column_permutation():
    g, h = GROUP_WIDTH, HEADS
    widths = (g,) * 4 + (g,) * 3 + (g,) * 4 + (h,) + (g,) * 4 + (h,) * 2
    starts = np.concatenate([[0], np.cumsum(widths)[:-1]])
    (hq, hf, hi, hg, dq, dk, dv, fq, fk, fv, fg, ff, mq, mk, mv, mo, mi, mf) = range(18)
    order_full = [hq, hf, hi, hg, dq, dk, dv, fq, fk, fv, fg, mq, mk, mv, mo]
    cols = [np.arange(starts[i], starts[i] + widths[i]) for i in order_full]
    gates = np.full((GATE_LANES,), -1, np.int64)
    gates[ROW_FOX_F:ROW_FOX_F + h] = np.arange(starts[ff], starts[ff] + h)
    gates[ROW_ML_B:ROW_ML_B + h] = np.arange(starts[mf], starts[mf] + h)
    gates[ROW_ML_I:ROW_ML_I + h] = np.arange(starts[mi], starts[mi] + h)
    return np.concatenate(cols + [gates])


def _split3(x):
    hi = x.astype(BF16)
    r1 = x - hi.astype(F32)
    mid = r1.astype(BF16)
    lo = (r1 - mid.astype(F32)).astype(BF16)
    return hi, mid, lo


def _split2(x):
    hi = x.astype(BF16)
    lo = (x - hi.astype(F32)).astype(BF16)
    return hi, lo


def _segment_sum(x, ones_bd):
    hi, lo = _split2(x)
    return (jnp.dot(hi, ones_bd, preferred_element_type=F32)
            + jnp.dot(lo, ones_bd, preferred_element_type=F32))


def _block_diag_ones(n, seg):
    idx = np.arange(n) // seg
    return jnp.asarray((idx[:, None] == idx[None, :]).astype(np.float32), dtype=BF16)


def _ada_kernel(c_ref, w_ref, b_ref, o_ref):
    c = c_ref[...]
    c_act = c * jax.nn.sigmoid(c)
    o_ref[0] = jnp.dot(c_act, w_ref[0], preferred_element_type=F32) + b_ref[0]


def _ada_modulation(c, w_ada, b_ada):
    depth, d, n = w_ada.shape
    batch = c.shape[0]
    bn = 1024
    return pl.pallas_call(
        _ada_kernel,
        grid=(depth, n // bn),
        in_specs=[
            pl.BlockSpec((batch, d), lambda l, j: (0, 0)),
            pl.BlockSpec((1, d, bn), lambda l, j: (l, 0, j)),
            pl.BlockSpec((1, 1, bn), lambda l, j: (l, 0, j)),
        ],
        out_specs=pl.BlockSpec((1, batch, bn), lambda l, j: (l, 0, j)),
        out_shape=jax.ShapeDtypeStruct((depth, batch, n), F32),
        compiler_params=_params(("parallel", "parallel")),
        name="ada_modulation",
    )(c, w_ada, b_ada.reshape(depth, 1, n))


def _modulated_norm(x, gain, shift, scale):
    ms = jnp.mean(x * x, axis=-1, keepdims=True)
    return x * lax.rsqrt(ms + NORM_EPS) * gain * (1.0 + scale) + shift


def _inproj_kernel(x_ref, gain_ref, shift_ref, scale_ref, w_ref, z_ref, *, col_chunk):
    h = _modulated_norm(x_ref[0], gain_ref[...], shift_ref[...], scale_ref[...]).astype(BF16)
    for j in range(0, Z_COLS, col_chunk):
        n = min(col_chunk, Z_COLS - j)
        z_ref[0, :, j:j + n] = jnp.dot(h, w_ref[:, j:j + n], preferred_element_type=F32)


def _mod_spec(layer, which):
    return pl.BlockSpec((None, None, None, 1, D_MODEL), lambda b, i: (layer, b, which, 0, 0))


def _input_projection(x, mod5, layer, gain, w_perm, tm):
    batch, seq, d = x.shape
    return pl.pallas_call(
        functools.partial(_inproj_kernel, col_chunk=512),
        grid=(batch, seq // tm),
        in_specs=[
            pl.BlockSpec((1, tm, d), lambda b, i: (b, i, 0)),
            pl.BlockSpec((1, d), lambda b, i: (0, 0)),
            _mod_spec(layer, 0),
            _mod_spec(layer, 1),
            pl.BlockSpec((d, Z_COLS), lambda b, i: (0, 0)),
        ],
        out_specs=pl.BlockSpec((1, tm, Z_COLS), lambda b, i: (b, i, 0)),
        out_shape=jax.ShapeDtypeStruct((batch, seq, Z_COLS), F32),
        compiler_params=_params(("parallel", "parallel")),
        name="input_projection",
    )(x, gain.reshape(1, d), mod5, mod5, w_perm)


def _ffn_kernel(x_ref, ya_ref, yb_ref, yc_ref, yd_ref, gate1_ref, gain_ref, shift_ref, scale_ref,
                gate2_ref, wo_ref, w1_ref, w2_ref, o_ref, *, ff_chunk):
    g = GROUP_WIDTH
    y = jnp.dot(ya_ref[0], wo_ref[0:g, :], preferred_element_type=F32)
    y += jnp.dot(yb_ref[0], wo_ref[g:2 * g, :], preferred_element_type=F32)
    y += jnp.dot(yc_ref[0], wo_ref[2 * g:3 * g, :], preferred_element_type=F32)
    y += jnp.dot(yd_ref[0], wo_ref[3 * g:4 * g, :], preferred_element_type=F32)
    x1 = x_ref[0] + gate1_ref[...] * y
    h = _modulated_norm(x1, gain_ref[...], shift_ref[...], scale_ref[...]).astype(BF16)
    acc = jnp.zeros_like(x1)
    for j in range(0, D_FF, ff_chunk):
        u = jnp.dot(h, w1_ref[:, j:j + ff_chunk], preferred_element_type=F32)
        u = jnp.square(jnp.maximum(u, 0.0)).astype(BF16)
        acc += jnp.dot(u, w2_ref[j:j + ff_chunk, :], preferred_element_type=F32)
    o_ref[0] = x1 + gate2_ref[...] * acc


def _out_projection_mlp(x, ys, mod5, layer, gain, w_out, w1, w2, tm):
    batch, seq, d = x.shape
    g = GROUP_WIDTH
    y_spec = pl.BlockSpec((1, tm, g), lambda b, i: (b, i, 0))
    const = lambda b, i: (0, 0)
    return pl.pallas_call(
        functools.partial(_ffn_kernel, ff_chunk=512),
        grid=(batch, seq // tm),
        in_specs=[
            pl.BlockSpec((1, tm, d), lambda b, i: (b, i, 0)),
            y_spec, y_spec, y_spec, y_spec,
            _mod_spec(layer, 2),
            pl.BlockSpec((1, d), const),
            _mod_spec(layer, 3),
            _mod_spec(layer, 4),
            _mod_spec(layer, 5),
            pl.BlockSpec((d, d), const, pipeline_mode=pl.Buffered(1)),
            pl.BlockSpec((d, D_FF), const, pipeline_mode=pl.Buffered(1)),
            pl.BlockSpec((D_FF, d), const, pipeline_mode=pl.Buffered(1)),
        ],
        out_specs=pl.BlockSpec((1, tm, d), lambda b, i: (b, i, 0)),
        out_shape=jax.ShapeDtypeStruct((batch, seq, d), F32),
        compiler_params=_params(("parallel", "parallel")),
        name="out_projection_mlp",
    )(x, *ys, mod5, gain.reshape(1, d), mod5, mod5, mod5, w_out, w1, w2)


def _log_sigmoid(t):
    return jnp.minimum(t, 0.0) - jnp.log1p(jnp.exp(-jnp.abs(t)))


def _segment_rms(x, gain, ones_bd, seg):
    ss = _segment_sum(x * x, ones_bd)
    return x * lax.rsqrt(ss * (1.0 / seg) + NORM_EPS) * gain


def _split3_f32(x):
    hi = x.astype(BF16).astype(F32)
    r1 = x - hi
    mid = r1.astype(BF16).astype(F32)
    lo = (r1 - mid).astype(BF16).astype(F32)
    return hi, mid, lo


def _prep_kernel(slopes_ref, dq_ref, dk_ref, dv_ref, fq_ref, fk_ref, fv_ref, zg_ref,
                 dqn_ref, dkn_ref, fqn_ref, fkn_ref, gbias_ref, bd32_ref, bd64_ref,
                 dqt_o, dk_o, dvt_o, fqt_o, fk_o, fvt_o, grow_o, mcol_o, carry_ref, mcarry_ref, *, tile):
    bd32 = bd32_ref[...]
    bd64 = bd64_ref[...]
    t0 = pl.program_id(1) * tile

    @pl.when(pl.program_id(1) == 0)
    def _():
        carry_ref[...] = jnp.zeros_like(carry_ref)
        mcarry_ref[...] = jnp.zeros_like(mcarry_ref)

    t = zg_ref[0].T[0:GATE_ROWS, :] + gbias_ref[...]
    row = lax.broadcasted_iota(jnp.int32, (GATE_ROWS, tile), 0)
    lane = lax.broadcasted_iota(jnp.int32, (GATE_ROWS, tile), 1)
    whole_seq = row < ROW_ML_B
    per_chunk = (row >= ROW_ML_B) & (row < ROW_ML_I)
    acc = jnp.where(whole_seq | per_chunk, _log_sigmoid(t), 0.0)
    lane_in_chunk = lane & (CHUNK - 1)
    shift = 1
    while shift < tile:
        ok = whole_seq & (lane >= shift)
        if shift < CHUNK:
            ok = ok | (per_chunk & (lane_in_chunk >= shift))
        acc = acc + jnp.where(ok, pltpu.roll(acc, shift, 1), 0.0)
        shift *= 2
    acc = acc + jnp.where(whole_seq, carry_ref[:, 0:1], 0.0)
    carry_ref[...] = jnp.broadcast_to(acc[:, tile - 1:tile], carry_ref.shape)
    g_all = t - pltpu.roll(acc, ROW_ML_G - ROW_ML_B, 0)
    run_max = g_all
    shift = 1
    while shift < CHUNK:
        run_max = jnp.maximum(run_max, jnp.where(lane_in_chunk >= shift, pltpu.roll(run_max, shift, 1), -jnp.inf))
        shift *= 2
    grow_o[0] = jnp.where(row < ROW_ML_G, acc,
                          jnp.where(row < ROW_ML_CG, g_all, pltpu.roll(run_max, ROW_ML_CG - ROW_ML_G, 0)))

    def over_chunk(x, op):
        step = 1
        while step < CHUNK:
            x = op(x, jnp.where(lane_in_chunk < CHUNK - step, pltpu.roll(x, tile - step, 1), x))
            step *= 2
        return x

    b_g = pltpu.roll(acc, ROW_ML_G - ROW_ML_B, 0)
    b_last = over_chunk(b_g, jnp.minimum)
    g_max = over_chunk(run_max, jnp.maximum)
    a_map, d_map = b_last, g_max + b_last
    step = CHUNK
    while step < tile:
        earlier = lane >= step
        d_map = jnp.where(earlier, jnp.maximum(pltpu.roll(d_map, step, 1) + a_map, d_map), d_map)
        a_map = jnp.where(earlier, pltpu.roll(a_map, step, 1) + a_map, a_map)
        step *= 2
    m_tile = mcarry_ref[:, 0:1]
    m_after = jnp.maximum(m_tile + a_map, d_map)
    m_before = jnp.where(lane >= CHUNK, pltpu.roll(m_after, CHUNK, 1), m_tile)
    mcarry_ref[...] = jnp.broadcast_to(m_after[:, tile - 1:tile], mcarry_ref.shape)
    big_m = jnp.maximum(m_before, run_max)
    w_state = jnp.exp(m_before - big_m)
    floor = jnp.exp(-(b_g + big_m))
    w_key = jnp.exp(g_all - jnp.maximum(m_before, g_max))
    assert COL_ML_FLOOR == ROW_ML_G
    mcol_o[0] = jnp.where(
        row < COL_ML_W_STATE, pltpu.roll(big_m, GATE_ROWS + COL_ML_M - ROW_ML_G, 0),
        jnp.where(row < COL_ML_FLOOR, pltpu.roll(w_state, GATE_ROWS + COL_ML_W_STATE - ROW_ML_G, 0),
                  jnp.where(row < COL_ML_W_KEY, floor, pltpu.roll(w_key, COL_ML_W_KEY - ROW_ML_G, 0))))

    hd = HEAD_DIM
    lane_pos = (t0 + lax.broadcasted_iota(jnp.int32, (1, tile), 1)).astype(F32)
    feat_row = lax.broadcasted_iota(jnp.int32, (hd, tile), 0)
    feat_lane = lax.broadcasted_iota(jnp.int32, (tile, KEY_WIDTH), 1)
    sub_row = lax.broadcasted_iota(jnp.int32, (8, tile), 0)
    pad_rows = jnp.zeros((KEY_WIDTH - hd - 8, tile), F32)

    def bias_features(bias_row, bias_first):
        hi, mid, lo = _split3_f32(bias_row)
        parts = jnp.where(sub_row % 3 == 0, hi, jnp.where(sub_row % 3 == 1, mid, lo))
        first, second = sub_row < 3, (sub_row >= 3) & (sub_row < 6)
        in_bias, in_ones = (first, second) if bias_first else (second, first)
        return jnp.where(in_bias, parts, jnp.where(in_ones, 1.0, 0.0))

    def query_aug(bias_row):
        return jnp.concatenate([bias_features(bias_row, False), pad_rows], axis=0)

    def key_aug(k_all, h, bias_row):
        blk = k_all[:, (h // 2) * KEY_WIDTH:(h // 2 + 1) * KEY_WIDTH]
        k_head = pltpu.roll(blk, hd, 1) if h % 2 else blk
        extra = jnp.concatenate([jnp.zeros((hd, tile), F32), bias_features(bias_row, True), pad_rows], axis=0)
        return jnp.where(feat_lane < hd, k_head, extra.T)

    dq_t = (_segment_rms(dq_ref[0], dqn_ref[...], bd32, DIFF_QK_DIM) * (DIFF_QK_DIM ** -0.5 * LOG2E)).T
    dk = _segment_rms(dk_ref[0], dkn_ref[...], bd32, DIFF_QK_DIM)
    dv_t = dv_ref[0].T
    fq_t = (_segment_rms(fq_ref[0], fqn_ref[...], bd64, hd) * (hd ** -0.5 * LOG2E)).T
    fk = _segment_rms(fk_ref[0], fkn_ref[...], bd64, hd)
    fv_t = fv_ref[0].T
    ones_rows = jnp.ones((VT_ROWS - hd, tile), F32)
    for h in range(HEADS):
        rows = slice(h * hd, (h + 1) * hd)
        slope = slopes_ref[h] * LOG2E
        aug = query_aug(-slope * lane_pos)
        for comp in range(2):
            in_comp = (feat_row >= comp * DIFF_QK_DIM) & (feat_row < (comp + 1) * DIFF_QK_DIM)
            q_comp = jnp.where(in_comp, dq_t[rows], 0.0)
            dqt_o[0, h, comp] = jnp.concatenate([q_comp, aug], axis=0).astype(BF16)
        dk_o[0, h] = key_aug(dk, h, slope * lane_pos).astype(BF16)
        dvt_o[0, h] = jnp.concatenate([dv_t[rows], ones_rows], axis=0).astype(BF16)
        f_row = acc[ROW_FOX_F + h:ROW_FOX_F + h + 1, :] * LOG2E
        fqt_o[0, h] = jnp.concatenate([fq_t[rows], query_aug(f_row)], axis=0).astype(BF16)
        fk_o[0, h] = key_aug(fk, h, -f_row).astype(BF16)
        fvt_o[0, h] = jnp.concatenate([fv_t[rows], ones_rows], axis=0).astype(BF16)


def _attention_prep(z, slopes, layer_params, tile):
    batch, seq, _ = z.shape
    g = GROUP_WIDTH
    dqn, dkn, fqn, fkn, gbias, bd32, bd64 = layer_params

    def zspec(block):
        return pl.BlockSpec((1, tile, g), lambda b, i: (b, i, block))

    const = lambda b, i: (0, 0)
    qt_spec = pl.BlockSpec((1, HEADS, KEY_WIDTH, tile), lambda b, i: (b, 0, 0, i))
    qt2_spec = pl.BlockSpec((1, HEADS, 2, KEY_WIDTH, tile), lambda b, i: (b, 0, 0, 0, i))
    k_spec = pl.BlockSpec((1, HEADS, tile, KEY_WIDTH), lambda b, i: (b, 0, i, 0))
    vt_spec = pl.BlockSpec((1, HEADS, VT_ROWS, tile), lambda b, i: (b, 0, 0, i))
    qt_shape = jax.ShapeDtypeStruct((batch, HEADS, KEY_WIDTH, seq), BF16)
    qt2_shape = jax.ShapeDtypeStruct((batch, HEADS, 2, KEY_WIDTH, seq), BF16)
    k_shape = jax.ShapeDtypeStruct((batch, HEADS, seq, KEY_WIDTH), BF16)
    vt_shape = jax.ShapeDtypeStruct((batch, HEADS, VT_ROWS, seq), BF16)
    return pl.pallas_call(
        functools.partial(_prep_kernel, tile=tile),
        grid=(batch, seq // tile),
        in_specs=[
            pl.BlockSpec(memory_space=pltpu.SMEM),
            zspec(Z_DQ), zspec(Z_DK), zspec(Z_DV), zspec(Z_FQ), zspec(Z_FK), zspec(Z_FV),
            pl.BlockSpec((1, tile, GATE_LANES), lambda b, i: (b, i, Z_GATE_BLOCK)),
            pl.BlockSpec((1, g), const), pl.BlockSpec((1, g), const),
            pl.BlockSpec((1, g), const), pl.BlockSpec((1, g), const),
            pl.BlockSpec((GATE_ROWS, 1), const),
            pl.BlockSpec((g, g), const), pl.BlockSpec((g, g), const),
        ],
        out_specs=[qt2_spec, k_spec, vt_spec, qt_spec, k_spec, vt_spec,
                   pl.BlockSpec((1, GATE_ROWS, tile), lambda b, i: (b, 0, i)),
                   pl.BlockSpec((1, GATE_ROWS, tile), lambda b, i: (b, 0, i))],
        out_shape=[qt2_shape, k_shape, vt_shape, qt_shape, k_shape, vt_shape,
                   jax.ShapeDtypeStruct((batch, GATE_ROWS, seq), F32),
                   jax.ShapeDtypeStruct((batch, GATE_ROWS, seq), F32)],
        scratch_shapes=[pltpu.VMEM((GATE_ROWS, GATE_LANES), F32), pltpu.VMEM((GATE_ROWS, GATE_LANES), F32)],
        compiler_params=_params(("parallel", "arbitrary")),
        name="attention_prep",
    )(slopes, z, z, z, z, z, z, z, dqn, dkn, fqn, fkn, gbias, bd32, bd64)


def _attention_sweep(qt_of_head, k_ref, vt_ref, qi, tq, tk, ncomp, scratch):
    cols = ncomp * tq
    assert tq % tk == 0
    s_scr, p_scr, m_scr, a_scr, acc_scr = scratch
    m_scr[...] = jnp.full(m_scr.shape, -jnp.inf, F32)
    a_scr[...] = jnp.ones(a_scr.shape, F32)
    acc_scr[...] = jnp.zeros(acc_scr.shape, F32)
    p_scr[1] = jnp.zeros(p_scr.shape[1:], BF16)
    key_minus_query = (lax.broadcasted_iota(jnp.int32, (tk, cols), 0)
                       - (lax.broadcasted_iota(jnp.int32, (tk, cols), 1) & (tq - 1)))

    def logits(h, j):
        start = pl.multiple_of(j * tk, tk)
        return jnp.dot(k_ref[0, h, pl.ds(start, tk), :], qt_of_head(h), preferred_element_type=F32)

    def weighted_values(h, j, slot):
        start = pl.multiple_of(j * tk, tk)
        return jnp.dot(vt_ref[0, h, :, pl.ds(start, tk)], p_scr[slot, h], preferred_element_type=F32)

    def accumulate(pv):
        for h in range(HEADS):
            acc_scr[h] = a_scr[h] * acc_scr[h] + pv[h]

    def stage(j, slot, masked, prefetch, prev_j):
        pv = [weighted_values(h, prev_j, 1 - slot) for h in range(HEADS)]
        s_next = [logits(h, j + 1) for h in range(HEADS)] if prefetch else None
        alphas = []
        for h in range(HEADS):
            s = s_scr[slot, h]
            if masked:
                s = jnp.where(key_minus_query <= qi * tq - j * tk, s, -jnp.inf)
            m_prev = m_scr[h]
            m_new = jnp.maximum(m_prev, jnp.max(s, axis=0, keepdims=True))
            alphas.append(jnp.exp2(m_prev - m_new))
            p_scr[slot, h] = jnp.exp2(s - m_new).astype(BF16)
            m_scr[h] = m_new
        accumulate(pv)
        for h in range(HEADS):
            a_scr[h] = alphas[h]
            if prefetch:
                s_scr[1 - slot, h] = s_next[h]

    for h in range(HEADS):
        s_scr[0, h] = logits(h, 0)
    per_block = tq // tk
    n_full = per_block * qi

    def slot_of(j, d):
        return d % 2 if per_block % 2 == 0 else j & 1

    def full_block(i, carry):
        for d in range(per_block):
            j = i * per_block + d
            stage(j, slot_of(j, d), False, True, jnp.maximum(j - 1, 0))
        return carry

    lax.fori_loop(0, qi, full_block, 0)
    for d in range(per_block):
        j = n_full + d
        stage(j, slot_of(j, d), True, d + 1 < per_block, jnp.maximum(j - 1, 0))
    j_last = n_full + per_block - 1
    accumulate([weighted_values(h, j_last, slot_of(j_last, per_block - 1)) for h in range(HEADS)])


def _diff_attn_kernel(lam_ref, qt_ref, k_ref, vt_ref, gain_ref, bd64_ref, o_ref, ot_scr, *scratch,
                      tq, tk, out_scale):
    qi = pl.program_id(1)
    lam = lam_ref[0]

    def qt_of_head(h):
        return jnp.concatenate([qt_ref[0, h, 0], qt_ref[0, h, 1]], axis=1)

    _attention_sweep(qt_of_head, k_ref, vt_ref, qi, tq, tk, 2, scratch)
    for h in range(HEADS):
        o = _normalised_output(scratch[-1], h)
        ot_scr[h * HEAD_DIM:(h + 1) * HEAD_DIM, :] = o[:, 0:tq] - lam * o[:, tq:2 * tq]
    y = _segment_rms(ot_scr[...].T, gain_ref[...] * out_scale, bd64_ref[...], HEAD_DIM)
    o_ref[0] = y.astype(BF16)


def _fox_attn_kernel(qt_ref, k_ref, vt_ref, g_ref, o_ref, ot_scr, *scratch, tq, tk):
    qi = pl.program_id(1)
    _attention_sweep(lambda h: qt_ref[0, h], k_ref, vt_ref, qi, tq, tk, 1, scratch)
    for h in range(HEADS):
        ot_scr[h * HEAD_DIM:(h + 1) * HEAD_DIM, :] = _normalised_output(scratch[-1], h)
    o_ref[0] = (ot_scr[...].T * jax.nn.sigmoid(g_ref[0])).astype(BF16)


def _attn_scratch(tq, tk, ncomp):
    cols = ncomp * tq
    return [
        pltpu.VMEM((GROUP_WIDTH, tq), F32),
        pltpu.VMEM((2, HEADS, tk, cols), F32),
        pltpu.VMEM((2, HEADS, tk, cols), BF16),
        pltpu.VMEM((HEADS, 1, cols), F32),
        pltpu.VMEM((HEADS, 1, cols), F32),
        pltpu.VMEM((HEADS, VT_ROWS, cols), F32),
    ]


def _normalised_output(acc_scr, h):
    return acc_scr[h, 0:HEAD_DIM, :] / acc_scr[h, HEAD_DIM:HEAD_DIM + 1, :]


def _diff_attention(qt, k, vt, lam, gain, bd64, out_scale, tq, tk):
    batch, _, seq, _ = k.shape
    g = GROUP_WIDTH
    return pl.pallas_call(
        functools.partial(_diff_attn_kernel, tq=tq, tk=tk, out_scale=out_scale),
        grid=(batch, seq // tq),
        in_specs=[
            pl.BlockSpec(memory_space=pltpu.SMEM),
            pl.BlockSpec((1, HEADS, 2, KEY_WIDTH, tq), lambda b, i: (b, 0, 0, 0, i)),
            pl.BlockSpec((1, HEADS, seq, KEY_WIDTH), lambda b, i: (b, 0, 0, 0)),
            pl.BlockSpec((1, HEADS, VT_ROWS, seq), lambda b, i: (b, 0, 0, 0)),
            pl.BlockSpec((1, g), lambda b, i: (0, 0)),
            pl.BlockSpec((g, g), lambda b, i: (0, 0)),
        ],
        out_specs=pl.BlockSpec((1, tq, g), lambda b, i: (b, i, 0)),
        out_shape=jax.ShapeDtypeStruct((batch, seq, g), BF16),
        scratch_shapes=_attn_scratch(tq, tk, 2),
        compiler_params=_params(("parallel", "arbitrary")),
        name="diff_attention",
    )(lam, qt, k, vt, gain, bd64)


def _fox_attention(qt, k, vt, z, tq, tk):
    batch, _, seq, _ = k.shape
    g = GROUP_WIDTH
    return pl.pallas_call(
        functools.partial(_fox_attn_kernel, tq=tq, tk=tk),
        grid=(batch, seq // tq),
        in_specs=[
            pl.BlockSpec((1, HEADS, KEY_WIDTH, tq), lambda b, i: (b, 0, 0, i)),
            pl.BlockSpec((1, HEADS, seq, KEY_WIDTH), lambda b, i: (b, 0, 0, 0)),
            pl.BlockSpec((1, HEADS, VT_ROWS, seq), lambda b, i: (b, 0, 0, 0)),
            pl.BlockSpec((1, tq, g), lambda b, i: (b, i, Z_FG)),
        ],
        out_specs=pl.BlockSpec((1, tq, g), lambda b, i: (b, i, 0)),
        out_shape=jax.ShapeDtypeStruct((batch, seq, g), BF16),
        scratch_shapes=_attn_scratch(tq, tk, 1),
        compiler_params=_params(("parallel", "arbitrary")),
        name="fox_attention",
    )(qt, k, vt, z)


def _hgrn2_kernel(q_ref, f_ref, i_ref, g_ref, lb_ref, gain_ref, bd64_ref, tri_ref, mask_ref,
                  o_ref, st_ref, *, chunks):
    @pl.when(pl.program_id(1) == 0)
    def _():
        st_ref[...] = jnp.zeros_like(st_ref)

    lb = lb_ref[...]
    bd64 = bd64_ref[...]
    tri = tri_ref[...]
    sub = 8
    gw = GROUP_WIDTH
    groups = CHUNK // sub
    sub_idx = lax.broadcasted_iota(jnp.int32, (groups, sub, gw), 1)
    row = lax.broadcasted_iota(jnp.int32, (CHUNK, gw), 0)
    lane_head = lax.broadcasted_iota(jnp.int32, (CHUNK, gw), 1) // HEAD_DIM
    key_pos = lax.broadcasted_iota(jnp.int32, (CHUNK, gw), 1) % HEAD_DIM

    def stack_heads(x):
        return jnp.concatenate([jnp.where(lane_head == h, x, 0.0) for h in range(HEADS)], axis=0).astype(BF16)

    span = range(chunks)
    rows = [slice(c * CHUNK, (c + 1) * CHUNK) for c in span]
    q = [q_ref[0, r, :] for r in rows]
    v = [i_ref[0, r, :] for r in rows]
    kk, b = [], []
    for c in span:
        f = lb + (1.0 - lb) * jax.nn.sigmoid(f_ref[0, rows[c], :])
        kk.append(1.0 - f)
        hi, mid, lo = _split3(jnp.log(f) * LOG2E)
        b.append(jnp.dot(tri, hi, preferred_element_type=F32)
                 + jnp.dot(tri, mid, preferred_element_type=F32)
                 + jnp.dot(tri, lo, preferred_element_type=F32))

    o = []
    for c in span:
        q3, k3, b3, v3 = (t.reshape(groups, sub, gw) for t in (q[c], kk[c], b[c], v[c]))
        o3 = jnp.zeros((groups, sub, gw), F32)
        for lag in range(sub):
            if lag == 0:
                x = q3 * k3
                vd = v3
            else:
                kd = pltpu.roll(k3, lag, 1)
                bd = pltpu.roll(b3, lag, 1)
                vd = pltpu.roll(v3, lag, 1)
                x = jnp.where(sub_idx >= lag, q3 * kd * jnp.exp2(b3 - bd), 0.0)
            a = jnp.dot(x.reshape(CHUNK, gw).astype(BF16), bd64, preferred_element_type=F32)
            o3 = o3 + a.reshape(groups, sub, gw) * vd
        o.append(o3.reshape(CHUNK, gw))

    a_far = []
    for c in span:
        total = None
        for half in (sub, 2 * sub, 4 * sub):
            block = 2 * half
            ref_rows = jnp.concatenate(
                [jnp.broadcast_to(b[c][p + half - 1:p + half, :], (block, gw)) for p in range(0, CHUNK, block)],
                axis=0)
            upper = ((row // half) & 1) == 1
            e = jnp.exp2(jnp.where(upper, b[c] - ref_rows, ref_rows - b[c]))
            qs = jnp.where(upper, q[c] * e, 0.0).astype(BF16)
            ks = stack_heads(jnp.where(upper, 0.0, kk[c] * e))
            a_lvl = lax.dot_general(qs, ks, (((1,), (1,)), ((), ())), preferred_element_type=F32)
            if block < CHUNK:
                a_lvl = jnp.where(row // block == key_pos // block, a_lvl, 0.0)
            total = a_lvl if total is None else total + a_lvl
        a_far.append(total.astype(BF16))
    for c in span:
        o[c] = o[c] + jnp.dot(a_far[c], stack_heads(v[c]), preferred_element_type=F32)

    upd = []
    for c in span:
        kp = (kk[c] * jnp.exp2(b[c][CHUNK - 1:CHUNK, :] - b[c])).astype(BF16)
        upd.append(lax.dot_general(v[c].astype(BF16), kp, (((0,), (0,)), ((), ())),
                                   preferred_element_type=F32))
    st = st_ref[...]
    for c in span:
        qe = (q[c] * jnp.exp2(b[c])).astype(BF16)
        o[c] = o[c] + lax.dot_general(qe, st.astype(BF16), (((1,), (1,)), ((), ())),
                                      preferred_element_type=F32)
        st = mask_ref[...] * (st * jnp.exp2(b[c][CHUNK - 1:CHUNK, :]) + upd[c])
    st_ref[...] = st

    for c in span:
        y = _segment_rms(o[c], gain_ref[...], bd64, HEAD_DIM)
        g = g_ref[0, rows[c], :]
        o_ref[0, rows[c], :] = (y * (g * jax.nn.sigmoid(g))).astype(BF16)


def _hgrn2(z, lb, gain, bd64, tri, mask, tile):
    batch, seq, _ = z.shape
    g = GROUP_WIDTH

    def zspec(block):
        return pl.BlockSpec((1, tile, g), lambda b, i: (b, i, block))

    const = lambda b, i: (0, 0)
    return pl.pallas_call(
        functools.partial(_hgrn2_kernel, chunks=tile // CHUNK),
        grid=(batch, seq // tile),
        in_specs=[
            zspec(Z_HQ), zspec(Z_HF), zspec(Z_HI), zspec(Z_HG),
            pl.BlockSpec((1, g), const), pl.BlockSpec((1, g), const),
            pl.BlockSpec((g, g), const), pl.BlockSpec((CHUNK, CHUNK), const),
            pl.BlockSpec((g, g), const),
        ],
        out_specs=pl.BlockSpec((1, tile, g), lambda b, i: (b, i, 0)),
        out_shape=jax.ShapeDtypeStruct((batch, seq, g), BF16),
        scratch_shapes=[pltpu.VMEM((g, g), F32)],
        compiler_params=_params(("parallel", "arbitrary")),
        name="hgrn2",
    )(z, z, z, z, lb, gain, bd64, tri, mask)


def _stack_heads(x):
    lane_head = lax.broadcasted_iota(jnp.int32, x.shape, 1) // HEAD_DIM
    return jnp.concatenate([jnp.where(lane_head == h, x, 0.0) for h in range(HEADS)], axis=0).astype(BF16)


def _mlstm_kernel(q_ref, k_ref, v_ref, og_ref, gcol_ref, grow_ref, conv_ref, expand_ref, bd64_ref,
                  mask_ref, y_ref, prev_scr, c_scr, n_scr, *, tile):
    @pl.when(pl.program_id(1) == 0)
    def _():
        prev_scr[...] = jnp.zeros_like(prev_scr)
        c_scr[...] = jnp.zeros_like(c_scr)
        n_scr[...] = jnp.zeros_like(n_scr)

    g = GROUP_WIDTH
    halo = prev_scr.shape[0]
    raw = jnp.concatenate([q_ref[0], k_ref[0]], axis=1)
    padded = jnp.concatenate([prev_scr[...], raw], axis=0)
    w = conv_ref[...]
    conv = w[CONV_WIDTH - 1:CONV_WIDTH, :] * raw
    for back in range(1, CONV_WIDTH):
        tap = CONV_WIDTH - 1 - back
        conv = conv + w[tap:tap + 1, :] * pltpu.roll(padded, back, 0)[halo:]
    prev_scr[...] = raw[tile - halo:]
    act = conv * jax.nn.sigmoid(conv)
    qc = act[:, :g] * HEAD_DIM ** -0.5
    kc = act[:, g:]
    v = v_ref[0]

    bd64 = bd64_ref[...]
    expand = expand_ref[...]
    chunks = tile // CHUNK
    span = range(chunks)
    rows = [slice(c * CHUNK, (c + 1) * CHUNK) for c in span]
    t_idx = lax.broadcasted_iota(jnp.int32, (CHUNK, g), 0)
    s_idx = lax.broadcasted_iota(jnp.int32, (CHUNK, g), 1) % HEAD_DIM
    causal = s_idx <= t_idx

    m_b, w_state_b, floor_b, w_key_b, g_b = [], [], [], [], []
    for c in span:
        hi, mid, lo = _split3(gcol_ref[0, rows[c], :])
        wide = (jnp.dot(hi, expand, preferred_element_type=F32)
                + jnp.dot(mid, expand, preferred_element_type=F32)
                + jnp.dot(lo, expand, preferred_element_type=F32))
        for dst, col in ((m_b, COL_ML_M), (w_state_b, COL_ML_W_STATE), (floor_b, COL_ML_FLOOR),
                         (w_key_b, COL_ML_W_KEY)):
            dst.append(wide[:, col // HEADS * g:(col // HEADS + 1) * g])
        g_b.append(jnp.concatenate(
            [grow_ref[0, ROW_ML_G + h:ROW_ML_G + h + 1, rows[c]] for h in range(HEADS)], axis=1))

    qb = [qc[r].astype(BF16) for r in rows]
    v_stack = [_stack_heads(v[r]) for r in rows]
    scores = [lax.dot_general(qb[c], _stack_heads(kc[rows[c]]), (((1,), (1,)), ((), ())),
                              preferred_element_type=F32) for c in span]
    num_intra, den_intra = [], []
    for c in span:
        s = scores[c] * jnp.exp(jnp.where(causal, g_b[c] - m_b[c], -jnp.inf))
        num_intra.append(jnp.dot(s.astype(BF16), v_stack[c], preferred_element_type=F32))
        den_intra.append(_segment_sum(s, bd64))
    c_upd, n_upd = [], []
    for c in span:
        kw = kc[rows[c]] * w_key_b[c]
        c_upd.append(mask_ref[...] * lax.dot_general(kw.astype(BF16), v[rows[c]].astype(BF16),
                                                     (((0,), (0,)), ((), ())), preferred_element_type=F32))
        n_upd.append(jnp.sum(kw, axis=0, keepdims=True))

    cm = c_scr[...]
    nv = n_scr[...]
    for c in span:
        inter = jnp.dot(qb[c], cm.astype(BF16), preferred_element_type=F32)
        q_dot_n = _segment_sum(qc[rows[c]] * nv, bd64)
        num = num_intra[c] + w_state_b[c] * inter
        den = den_intra[c] + w_state_b[c] * q_dot_n
        y = num / jnp.maximum(jnp.abs(den), floor_b[c])
        y_ref[0, rows[c], :] = (jax.nn.sigmoid(og_ref[0, rows[c], :]) * y).astype(BF16)
        decay = w_state_b[c][CHUNK - 1:CHUNK, :]
        cm = decay * cm + c_upd[c]
        nv = decay * nv + n_upd[c]
    c_scr[...] = cm
    n_scr[...] = nv


def _mlstm(z, gcol, grow, conv_w, bd64, mask, tile):
    batch, seq, _ = z.shape
    g = GROUP_WIDTH
    expand = jnp.asarray(np.repeat(np.eye(GATE_ROWS, dtype=np.float32), HEAD_DIM, axis=1), dtype=BF16)

    def zspec(block):
        return pl.BlockSpec((1, tile, g), lambda b, i: (b, i, block))

    const = lambda b, i: (0, 0)
    return pl.pallas_call(
        functools.partial(_mlstm_kernel, tile=tile),
        grid=(batch, seq // tile),
        in_specs=[
            zspec(Z_MQ), zspec(Z_MK), zspec(Z_MV), zspec(Z_MO),
            pl.BlockSpec((1, tile, GATE_ROWS), lambda b, i: (b, i, 0)),
            pl.BlockSpec((1, GATE_ROWS, tile), lambda b, i: (b, 0, i)),
            pl.BlockSpec((CONV_WIDTH, 2 * g), const),
            pl.BlockSpec((GATE_ROWS, GATE_ROWS * HEAD_DIM), const),
            pl.BlockSpec((g, g), const),
            pl.BlockSpec((g, g), const),
        ],
        out_specs=pl.BlockSpec((1, tile, g), lambda b, i: (b, i, 0)),
        out_shape=jax.ShapeDtypeStruct((batch, seq, g), BF16),
        scratch_shapes=[
            pltpu.VMEM((8, 2 * g), F32),
            pltpu.VMEM((g, g), F32),
            pltpu.VMEM((1, g), F32),
        ],
        compiler_params=_params(("parallel", "arbitrary")),
        name="mlstm",
    )(z, z, z, z, gcol, grow, conv_w, expand, bd64, mask)


def _tile_rows(v, reps):
    return jnp.tile(v.astype(F32), reps).reshape(1, -1)


def kernel(x, c, w_ada, b_ada, norm_mix_gain, norm_ff_gain, w_in, w_out, hg_lb_logits, hg_norm_gain,
           diff_qn_gain, diff_kn_gain, diff_lambda, diff_sub_gain, fox_qn_gain, fox_kn_gain, fox_f_bias,
           mlstm_conv, mlstm_i_bias, mlstm_f_bias, w_ff1, w_ff2):
    depth = w_in.shape[0]
    batch, seq, d = x.shape
    g, hd = GROUP_WIDTH, HEAD_DIM
    tm = min(512, seq)
    tq = min(512, seq)
    tk = 128
    tr = min(512, seq)

    p_lb = jax.nn.softmax(hg_lb_logits.astype(F32), axis=0)
    lower_bounds = jnp.cumsum(p_lb, axis=0) - p_lb[0:1]
    perm = _in_column_permutation()
    perm_idx = jnp.asarray(np.maximum(perm, 0), jnp.int32)
    perm_valid = jnp.asarray(perm >= 0)
    bd32 = _block_diag_ones(g, DIFF_QK_DIM)
    bd64 = _block_diag_ones(g, hd)
    bd64_f32 = bd64.astype(F32)
    tri = jnp.asarray(np.tril(np.ones((CHUNK, CHUNK), np.float32)), dtype=BF16)
    slopes = jnp.asarray(2.0 ** (-8.0 * np.arange(1, HEADS + 1) / HEADS), F32)

    mod = _ada_modulation(c.astype(F32), w_ada, b_ada)
    mod5 = mod.reshape(depth, batch, 6, 1, d)

    for l in range(depth):
        w_perm = jnp.where(perm_valid[None, :], jnp.take(w_in[l], perm_idx, axis=1), 0.0).astype(BF16)
        z = _input_projection(x, mod5, l, norm_mix_gain[l], w_perm, tm)

        gbias = jnp.zeros((GATE_ROWS,), F32)
        gbias = gbias.at[ROW_FOX_F:ROW_FOX_F + HEADS].set(fox_f_bias[l].astype(F32))
        gbias = gbias.at[ROW_ML_B:ROW_ML_B + HEADS].set(mlstm_f_bias[l].astype(F32))
        gbias = gbias.at[ROW_ML_I:ROW_ML_I + HEADS].set(mlstm_i_bias[l].astype(F32))
        prep_params = (
            _tile_rows(diff_qn_gain[l], 2 * HEADS), _tile_rows(diff_kn_gain[l], 2 * HEADS),
            _tile_rows(fox_qn_gain[l], HEADS), _tile_rows(fox_kn_gain[l], HEADS),
            gbias.reshape(GATE_ROWS, 1), bd32, bd64,
        )
        dqt, dk, dvt, fqt, fk, fvt, grow, mcol_t = _attention_prep(z, slopes, prep_params, tm)
        gcol = jnp.transpose(mcol_t, (0, 2, 1))

        lam_init = 0.8 - 0.6 * math.exp(-0.3 * l)
        lv = diff_lambda[l].astype(F32)
        lam = jnp.exp(jnp.sum(lv[0] * lv[1])) - jnp.exp(jnp.sum(lv[2] * lv[3])) + lam_init

        y_a = _hgrn2(z, lower_bounds[l].reshape(1, g), _tile_rows(hg_norm_gain[l], HEADS),
                     bd64, tri, bd64_f32, tr)
        y_b = _diff_attention(dqt, dk, dvt, lam.reshape(1), _tile_rows(diff_sub_gain[l], HEADS),
                              bd64, 1.0 - lam_init, tq, tk)
        y_c = _fox_attention(fqt, fk, fvt, z, tq, tk)
        y_d = _mlstm(z, gcol, grow, mlstm_conv[l].astype(F32), bd64, bd64_f32, tr)

        x = _out_projection_mlp(x, (y_a, y_b, y_c, y_d), mod5, l, norm_ff_gain[l],
                                w_out[l].astype(BF16), w_ff1[l].astype(BF16), w_ff2[l].astype(BF16), tm)
    return x
```

```python
import functools
import math

import numpy as np
import jax
import jax.numpy as jnp
from jax import lax
from jax.experimental import pallas as pl
from jax.experimental.pallas import tpu as pltpu

F32 = jnp.float32
BF16 = jnp.bfloat16

D_MODEL = 1024
N_GROUPS = 4
GROUP_WIDTH = D_MODEL // N_GROUPS
HEADS = 4
HEAD_DIM = GROUP_WIDTH // HEADS
DIFF_QK_DIM = HEAD_DIM // 2
D_FF = 4 * D_MODEL
CHUNK = 64
CONV_WIDTH = 4
NORM_EPS = 1e-6
N_FULL_GROUPS = 15
GATE_LANES = 128
Z_COLS = N_FULL_GROUPS * GROUP_WIDTH + GATE_LANES
GATE_ROWS = 16
KEY_WIDTH = 128
VT_ROWS = HEAD_DIM + 16
LOG2E = math.log2(math.e)

(Z_HQ, Z_HF, Z_HI, Z_HG, Z_DQ, Z_DK, Z_DV, Z_FQ, Z_FK, Z_FV, Z_FG,
 Z_MQ, Z_MK, Z_MV, Z_MO) = range(N_FULL_GROUPS)
Z_GATE_BLOCK = N_FULL_GROUPS * GROUP_WIDTH // GATE_LANES
ROW_FOX_F, ROW_ML_B, ROW_ML_G, ROW_ML_CG = 0, 4, 8, 12
ROW_ML_I = ROW_ML_G
COL_ML_M, COL_ML_W_STATE, COL_ML_FLOOR, COL_ML_W_KEY = 0, 4, 8, 12

VMEM_LIMIT = 56 * 1024 * 1024


def _params(sem):
    return pltpu.CompilerParams(dimension_semantics=sem, vmem_limit_bytes=VMEM_LIMIT)


def _permuted_in_weights(w):
    g, h = GROUP_WIDTH, HEADS
    fox_f = 11 * g
    ml_q = fox_f + h
    ml_i = ml_q + 4 * g
    ml_f = ml_i + h
    assert (ROW_FOX_F, ROW_ML_B, ROW_ML_I) == (0, h, 2 * h) and w.shape[1] == ml_f + h
    pieces = [w[:, :fox_f], w[:, ml_q:ml_i], w[:, fox_f:ml_q], w[:, ml_f:ml_f + h], w[:, ml_i:ml_f]]
    pad = jnp.zeros((w.shape[0], GATE_LANES - 3 * h), BF16)
    return jnp.concatenate([p.astype(BF16) for p in pieces] + [pad], axis=1)


def _split3(x):
    hi = x.astype(BF16)
    r1 = x - hi.astype(F32)
    mid = r1.astype(BF16)
    lo = (r1 - mid.astype(F32)).astype(BF16)
    return hi, mid, lo


def _split2(x):
    hi = x.astype(BF16)
    lo = (x - hi.astype(F32)).astype(BF16)
    return hi, lo


def _segment_sum(x, ones_bd):
    hi, lo = _split2(x)
    return (jnp.dot(hi, ones_bd, preferred_element_type=F32)
            + jnp.dot(lo, ones_bd, preferred_element_type=F32))


def _block_diag_ones(n, seg):
    idx = np.arange(n) // seg
    return jnp.asarray((idx[:, None] == idx[None, :]).astype(np.float32), dtype=BF16)


def _ada_kernel(c_ref, w_ref, b_ref, o_ref):
    c = c_ref[...]
    c_act = c * jax.nn.sigmoid(c)
    o_ref[0] = jnp.dot(c_act, w_ref[0], preferred_element_type=F32) + b_ref[0]


def _ada_modulation(c, w_ada, b_ada):
    depth, d, n = w_ada.shape
    batch = c.shape[0]
    bn = 1024
    return pl.pallas_call(
        _ada_kernel,
        grid=(depth, n // bn),
        in_specs=[
            pl.BlockSpec((batch, d), lambda l, j: (0, 0)),
            pl.BlockSpec((1, d, bn), lambda l, j: (l, 0, j)),
            pl.BlockSpec((1, 1, bn), lambda l, j: (l, 0, j)),
        ],
        out_specs=pl.BlockSpec((1, batch, bn), lambda l, j: (l, 0, j)),
        out_shape=jax.ShapeDtypeStruct((depth, batch, n), F32),
        compiler_params=_params(("parallel", "parallel")),
        name="ada_modulation",
    )(c, w_ada, b_ada.reshape(depth, 1, n))


def _modulated_norm(x, gain, shift, scale):
    ms = jnp.mean(x * x, axis=-1, keepdims=True)
    return x * lax.rsqrt(ms + NORM_EPS) * gain * (1.0 + scale) + shift


def _inproj_kernel(x_ref, gain_ref, shift_ref, scale_ref, w_ref, z_ref, *, col_chunk):
    h = _modulated_norm(x_ref[0], gain_ref[...], shift_ref[...], scale_ref[...]).astype(BF16)
    for j in range(0, Z_COLS, col_chunk):
        n = min(col_chunk, Z_COLS - j)
        z_ref[0, :, j:j + n] = jnp.dot(h, w_ref[:, j:j + n], preferred_element_type=F32)


def _mod_spec(layer, which):
    return pl.BlockSpec((None, None, None, 1, D_MODEL), lambda b, i: (layer, b, which, 0, 0))


def _input_projection(x, mod5, layer, gain, w_perm, tm):
    batch, seq, d = x.shape
    return pl.pallas_call(
        functools.partial(_inproj_kernel, col_chunk=512),
        grid=(batch, seq // tm),
        in_specs=[
            pl.BlockSpec((1, tm, d), lambda b, i: (b, i, 0)),
            pl.BlockSpec((1, d), lambda b, i: (0, 0)),
            _mod_spec(layer, 0),
            _mod_spec(layer, 1),
            pl.BlockSpec((d, Z_COLS), lambda b, i: (0, 0)),
        ],
        out_specs=pl.BlockSpec((1, tm, Z_COLS), lambda b, i: (b, i, 0)),
        out_shape=jax.ShapeDtypeStruct((batch, seq, Z_COLS), F32),
        compiler_params=_params(("parallel", "parallel")),
        name="input_projection",
    )(x, gain.reshape(1, d), mod5, mod5, w_perm)


def _ffn_kernel(x_ref, ya_ref, yb_ref, yc_ref, yd_ref, gate1_ref, gain_ref, shift_ref, scale_ref,
                gate2_ref, wo_ref, w1_ref, w2_ref, o_ref, *, ff_chunk):
    g = GROUP_WIDTH
    y = jnp.dot(ya_ref[0], wo_ref[0:g, :], preferred_element_type=F32)
    y += jnp.dot(yb_ref[0], wo_ref[g:2 * g, :], preferred_element_type=F32)
    y += jnp.dot(yc_ref[0], wo_ref[2 * g:3 * g, :], preferred_element_type=F32)
    y += jnp.dot(yd_ref[0], wo_ref[3 * g:4 * g, :], preferred_element_type=F32)
    x1 = x_ref[0] + gate1_ref[...] * y
    h = _modulated_norm(x1, gain_ref[...], shift_ref[...], scale_ref[...]).astype(BF16)
    acc = jnp.zeros_like(x1)
    for j in range(0, D_FF, ff_chunk):
        u = jnp.dot(h, w1_ref[:, j:j + ff_chunk], preferred_element_type=F32)
        u = jnp.square(jnp.maximum(u, 0.0)).astype(BF16)
        acc += jnp.dot(u, w2_ref[j:j + ff_chunk, :], preferred_element_type=F32)
    o_ref[0] = x1 + gate2_ref[...] * acc


def _out_projection_mlp(x, ys, mod5, layer, gain, w_out, w1, w2, tm):
    batch, seq, d = x.shape
    g = GROUP_WIDTH
    y_spec = pl.BlockSpec((1, tm, g), lambda b, i: (b, i, 0))
    const = lambda b, i: (0, 0)
    return pl.pallas_call(
        functools.partial(_ffn_kernel, ff_chunk=512),
        grid=(batch, seq // tm),
        in_specs=[
            pl.BlockSpec((1, tm, d), lambda b, i: (b, i, 0)),
            y_spec, y_spec, y_spec, y_spec,
            _mod_spec(layer, 2),
            pl.BlockSpec((1, d), const),
            _mod_spec(layer, 3),
            _mod_spec(layer, 4),
            _mod_spec(layer, 5),
            pl.BlockSpec((d, d), const, pipeline_mode=pl.Buffered(1)),
            pl.BlockSpec((d, D_FF), const, pipeline_mode=pl.Buffered(1)),
            pl.BlockSpec((D_FF, d), const, pipeline_mode=pl.Buffered(1)),
        ],
        out_specs=pl.BlockSpec((1, tm, d), lambda b, i: (b, i, 0)),
        out_shape=jax.ShapeDtypeStruct((batch, seq, d), F32),
        compiler_params=_params(("parallel", "parallel")),
        name="out_projection_mlp",
    )(x, *ys, mod5, gain.reshape(1, d), mod5, mod5, mod5, w_out, w1, w2)


def _log_sigmoid(t):
    return jnp.minimum(t, 0.0) - jnp.log1p(jnp.exp(-jnp.abs(t)))


def _segment_rms(x, gain, ones_bd, seg):
    ss = jnp.dot((x * x).astype(BF16), ones_bd, preferred_element_type=F32)
    return x * lax.rsqrt(ss * (1.0 / seg) + NORM_EPS) * gain


def _split3_f32(x):
    hi = x.astype(BF16).astype(F32)
    r1 = x - hi
    mid = r1.astype(BF16).astype(F32)
    lo = (r1 - mid).astype(BF16).astype(F32)
    return hi, mid, lo


def _prep_kernel(slopes_ref, dq_ref, dk_ref, dv_ref, fq_ref, fk_ref, fv_ref, zg_ref,
                 dqn_ref, dkn_ref, fqn_ref, fkn_ref, gbias_ref, bd32_ref, bd64_ref,
                 dqt_o, dk_o, dvt_o, fqt_o, fk_o, fvt_o, grow_o, mcol_o, carry_ref, mcarry_ref, *, tile):
    bd32 = bd32_ref[...]
    bd64 = bd64_ref[...]
    t0 = pl.program_id(1) * tile

    @pl.when(pl.program_id(1) == 0)
    def _():
        carry_ref[...] = jnp.zeros_like(carry_ref)
        mcarry_ref[...] = jnp.zeros_like(mcarry_ref)

    t = zg_ref[0].T[0:GATE_ROWS, :] + gbias_ref[...]
    row = lax.broadcasted_iota(jnp.int32, (GATE_ROWS, tile), 0)
    lane = lax.broadcasted_iota(jnp.int32, (GATE_ROWS, tile), 1)
    whole_seq = row < ROW_ML_B
    per_chunk = (row >= ROW_ML_B) & (row < ROW_ML_I)
    acc = jnp.where(whole_seq | per_chunk, _log_sigmoid(t), 0.0)
    lane_in_chunk = lane & (CHUNK - 1)
    shift = 1
    while shift < tile:
        ok = whole_seq & (lane >= shift)
        if shift < CHUNK:
            ok = ok | (per_chunk & (lane_in_chunk >= shift))
        acc = acc + jnp.where(ok, pltpu.roll(acc, shift, 1), 0.0)
        shift *= 2
    acc = acc + jnp.where(whole_seq, carry_ref[:, 0:1], 0.0)
    carry_ref[...] = jnp.broadcast_to(acc[:, tile - 1:tile], carry_ref.shape)
    g_all = t - pltpu.roll(acc, ROW_ML_G - ROW_ML_B, 0)
    run_max = g_all
    shift = 1
    while shift < CHUNK:
        run_max = jnp.maximum(run_max, jnp.where(lane_in_chunk >= shift, pltpu.roll(run_max, shift, 1), -jnp.inf))
        shift *= 2
    grow_o[0] = jnp.where(row < ROW_ML_G, acc,
                          jnp.where(row < ROW_ML_CG, g_all, pltpu.roll(run_max, ROW_ML_CG - ROW_ML_G, 0)))

    def over_chunk(x, op):
        step = 1
        while step < CHUNK:
            x = op(x, jnp.where(lane_in_chunk < CHUNK - step, pltpu.roll(x, tile - step, 1), x))
            step *= 2
        return x

    b_g = pltpu.roll(acc, ROW_ML_G - ROW_ML_B, 0)
    b_last = over_chunk(b_g, jnp.minimum)
    g_max = over_chunk(run_max, jnp.maximum)
    a_map, d_map = b_last, g_max + b_last
    step = CHUNK
    while step < tile:
        earlier = lane >= step
        d_map = jnp.where(earlier, jnp.maximum(pltpu.roll(d_map, step, 1) + a_map, d_map), d_map)
        a_map = jnp.where(earlier, pltpu.roll(a_map, step, 1) + a_map, a_map)
        step *= 2
    m_tile = mcarry_ref[:, 0:1]
    m_after = jnp.maximum(m_tile + a_map, d_map)
    m_before = jnp.where(lane >= CHUNK, pltpu.roll(m_after, CHUNK, 1), m_tile)
    mcarry_ref[...] = jnp.broadcast_to(m_after[:, tile - 1:tile], mcarry_ref.shape)
    big_m = jnp.maximum(m_before, run_max)
    w_state = jnp.exp(m_before - big_m)
    floor = jnp.exp(-(b_g + big_m))
    w_key = jnp.exp(g_all - jnp.maximum(m_before, g_max))
    assert COL_ML_FLOOR == ROW_ML_G
    mcol_o[0] = jnp.where(
        row < COL_ML_W_STATE, pltpu.roll(big_m, GATE_ROWS + COL_ML_M - ROW_ML_G, 0),
        jnp.where(row < COL_ML_FLOOR, pltpu.roll(w_state, GATE_ROWS + COL_ML_W_STATE - ROW_ML_G, 0),
                  jnp.where(row < COL_ML_W_KEY, floor, pltpu.roll(w_key, COL_ML_W_KEY - ROW_ML_G, 0))))

    hd = HEAD_DIM
    lane_pos = (t0 + lax.broadcasted_iota(jnp.int32, (1, tile), 1)).astype(F32)
    feat_row = lax.broadcasted_iota(jnp.int32, (hd, tile), 0)
    feat_lane = lax.broadcasted_iota(jnp.int32, (tile, KEY_WIDTH), 1)
    sub_row = lax.broadcasted_iota(jnp.int32, (8, tile), 0)
    pad_rows = jnp.zeros((KEY_WIDTH - hd - 8, tile), F32)

    def bias_features(bias_row, bias_first):
        hi, mid, lo = _split3_f32(bias_row)
        parts = jnp.where(sub_row % 3 == 0, hi, jnp.where(sub_row % 3 == 1, mid, lo))
        first, second = sub_row < 3, (sub_row >= 3) & (sub_row < 6)
        in_bias, in_ones = (first, second) if bias_first else (second, first)
        return jnp.where(in_bias, parts, jnp.where(in_ones, 1.0, 0.0))

    def query_aug(bias_row):
        return jnp.concatenate([bias_features(bias_row, False), pad_rows], axis=0)

    def key_aug(k_all, h, bias_row):
        blk = k_all[:, (h // 2) * KEY_WIDTH:(h // 2 + 1) * KEY_WIDTH]
        k_head = pltpu.roll(blk, hd, 1) if h % 2 else blk
        extra = jnp.concatenate([jnp.zeros((hd, tile), F32), bias_features(bias_row, True), pad_rows], axis=0)
        return jnp.where(feat_lane < hd, k_head, extra.T)

    dq_t = (_segment_rms(dq_ref[0], dqn_ref[...], bd32, DIFF_QK_DIM) * (DIFF_QK_DIM ** -0.5 * LOG2E)).T
    dk = _segment_rms(dk_ref[0], dkn_ref[...], bd32, DIFF_QK_DIM)
    dv_t = dv_ref[0].T
    fq_t = (_segment_rms(fq_ref[0], fqn_ref[...], bd64, hd) * (hd ** -0.5 * LOG2E)).T
    fk = _segment_rms(fk_ref[0], fkn_ref[...], bd64, hd)
    fv_t = fv_ref[0].T
    ones_rows = jnp.ones((VT_ROWS - hd, tile), F32)
    for h in range(HEADS):
        rows = slice(h * hd, (h + 1) * hd)
        slope = slopes_ref[h] * LOG2E
        aug = query_aug(-slope * lane_pos)
        for comp in range(2):
            in_comp = (feat_row >= comp * DIFF_QK_DIM) & (feat_row < (comp + 1) * DIFF_QK_DIM)
            q_comp = jnp.where(in_comp, dq_t[rows], 0.0)
            dqt_o[0, h, comp] = jnp.concatenate([q_comp, aug], axis=0).astype(BF16)
        dk_o[0, h] = key_aug(dk, h, slope * lane_pos).astype(BF16)
        dvt_o[0, h] = jnp.concatenate([dv_t[rows], ones_rows], axis=0).astype(BF16)
        f_row = acc[ROW_FOX_F + h:ROW_FOX_F + h + 1, :] * LOG2E
        fqt_o[0, h] = jnp.concatenate([fq_t[rows], query_aug(f_row)], axis=0).astype(BF16)
        fk_o[0, h] = key_aug(fk, h, -f_row).astype(BF16)
        fvt_o[0, h] = jnp.concatenate([fv_t[rows], ones_rows], axis=0).astype(BF16)


def _attention_prep(z, slopes, layer_params, tile):
    batch, seq, _ = z.shape
    g = GROUP_WIDTH
    dqn, dkn, fqn, fkn, gbias, bd32, bd64 = layer_params

    def zspec(block):
        return pl.BlockSpec((1, tile, g), lambda b, i: (b, i, block))

    const = lambda b, i: (0, 0)
    qt_spec = pl.BlockSpec((1, HEADS, KEY_WIDTH, tile), lambda b, i: (b, 0, 0, i))
    qt2_spec = pl.BlockSpec((1, HEADS, 2, KEY_WIDTH, tile), lambda b, i: (b, 0, 0, 0, i))
    k_spec = pl.BlockSpec((1, HEADS, tile, KEY_WIDTH), lambda b, i: (b, 0, i, 0))
    vt_spec = pl.BlockSpec((1, HEADS, VT_ROWS, tile), lambda b, i: (b, 0, 0, i))
    qt_shape = jax.ShapeDtypeStruct((batch, HEADS, KEY_WIDTH, seq), BF16)
    qt2_shape = jax.ShapeDtypeStruct((batch, HEADS, 2, KEY_WIDTH, seq), BF16)
    k_shape = jax.ShapeDtypeStruct((batch, HEADS, seq, KEY_WIDTH), BF16)
    vt_shape = jax.ShapeDtypeStruct((batch, HEADS, VT_ROWS, seq), BF16)
    return pl.pallas_call(
        functools.partial(_prep_kernel, tile=tile),
        grid=(batch, seq // tile),
        in_specs=[
            pl.BlockSpec(memory_space=pltpu.SMEM),
            zspec(Z_DQ), zspec(Z_DK), zspec(Z_DV), zspec(Z_FQ), zspec(Z_FK), zspec(Z_FV),
            pl.BlockSpec((1, tile, GATE_LANES), lambda b, i: (b, i, Z_GATE_BLOCK)),
            pl.BlockSpec((1, g), const), pl.BlockSpec((1, g), const),
            pl.BlockSpec((1, g), const), pl.BlockSpec((1, g), const),
            pl.BlockSpec((GATE_ROWS, 1), const),
            pl.BlockSpec((g, g), const), pl.BlockSpec((g, g), const),
        ],
        out_specs=[qt2_spec, k_spec, vt_spec, qt_spec, k_spec, vt_spec,
                   pl.BlockSpec((1, GATE_ROWS, tile), lambda b, i: (b, 0, i)),
                   pl.BlockSpec((1, GATE_ROWS, tile), lambda b, i: (b, 0, i))],
        out_shape=[qt2_shape, k_shape, vt_shape, qt_shape, k_shape, vt_shape,
                   jax.ShapeDtypeStruct((batch, GATE_ROWS, seq), F32),
                   jax.ShapeDtypeStruct((batch, GATE_ROWS, seq), F32)],
        scratch_shapes=[pltpu.VMEM((GATE_ROWS, GATE_LANES), F32), pltpu.VMEM((GATE_ROWS, GATE_LANES), F32)],
        compiler_params=_params(("parallel", "arbitrary")),
        name="attention_prep",
    )(slopes, z, z, z, z, z, z, z, dqn, dkn, fqn, fkn, gbias, bd32, bd64)


def _attention_sweep(qt_of_head, k_ref, vt_ref, qi, tq, tk, ncomp, scratch):
    cols = ncomp * tq
    assert tq % tk == 0
    s_scr, p_scr, m_scr, a_scr, acc_scr = scratch
    m_scr[...] = jnp.full(m_scr.shape, -jnp.inf, F32)
    a_scr[...] = jnp.ones(a_scr.shape, F32)
    acc_scr[...] = jnp.zeros(acc_scr.shape, F32)
    p_scr[1] = jnp.zeros(p_scr.shape[1:], BF16)
    per_block = tq // tk
    n_full = per_block * qi
    every = [(0, cols)]

    def visible(d):
        return [(c * tq + d * tk, (c + 1) * tq) for c in range(ncomp)]

    def logits(h, j, ranges):
        start = pl.multiple_of(j * tk, tk)
        k = k_ref[0, h, pl.ds(start, tk), :]
        qt = qt_of_head(h)
        return [jnp.dot(k, qt[:, lo:hi], preferred_element_type=F32) for lo, hi in ranges]

    def weighted_values(h, j, slot, ranges):
        start = pl.multiple_of(j * tk, tk)
        vt = vt_ref[0, h, :, pl.ds(start, tk)]
        return [jnp.dot(vt, p_scr[slot, h, :, lo:hi], preferred_element_type=F32) for lo, hi in ranges]

    def accumulate(pv, ranges):
        for h in range(HEADS):
            for (lo, hi), x in zip(ranges, pv[h]):
                acc_scr[h, :, lo:hi] = a_scr[h, :, lo:hi] * acc_scr[h, :, lo:hi] + x

    def stage(j, slot, prev_j, prev, cur, nxt, masked):
        pv = [weighted_values(h, prev_j, 1 - slot, prev) for h in range(HEADS)]
        s_next = [logits(h, j + 1, nxt) for h in range(HEADS)] if nxt else None
        alphas = []
        for h in range(HEADS):
            alphas.append([])
            for lo, hi in cur:
                s = s_scr[slot, h, :, lo:hi]
                if masked:
                    key = lax.broadcasted_iota(jnp.int32, (tk, hi - lo), 0)
                    query = (lax.broadcasted_iota(jnp.int32, (tk, hi - lo), 1) + lo) & (tq - 1)
                    s = jnp.where(key - query <= qi * tq - j * tk, s, -jnp.inf)
                m_prev = m_scr[h, :, lo:hi]
                m_new = jnp.maximum(m_prev, jnp.max(s, axis=0, keepdims=True))
                alphas[h].append(jnp.exp2(m_prev - m_new))
                p_scr[slot, h, :, lo:hi] = jnp.exp2(s - m_new).astype(BF16)
                m_scr[h, :, lo:hi] = m_new
        accumulate(pv, prev)
        for h in range(HEADS):
            for (lo, hi), alpha in zip(cur, alphas[h]):
                a_scr[h, :, lo:hi] = alpha
            if nxt:
                for (lo, hi), x in zip(nxt, s_next[h]):
                    s_scr[1 - slot, h, :, lo:hi] = x

    for h in range(HEADS):
        s_scr[0, h] = logits(h, 0, every)[0]

    def slot_of(j, d):
        return d % 2 if per_block % 2 == 0 else j & 1

    def full_block(i, carry):
        for d in range(per_block):
            j = i * per_block + d
            stage(j, slot_of(j, d), jnp.maximum(j - 1, 0), every, every, every, False)
        return carry

    lax.fori_loop(0, qi, full_block, 0)
    for d in range(per_block):
        j = n_full + d
        stage(j, slot_of(j, d), jnp.maximum(j - 1, 0), visible(d - 1) if d else every, visible(d),
              visible(d + 1) if d + 1 < per_block else None, True)
    j_last = n_full + per_block - 1
    last = visible(per_block - 1)
    accumulate([weighted_values(h, j_last, slot_of(j_last, per_block - 1), last) for h in range(HEADS)], last)


def _diff_attn_kernel(lam_ref, qt_ref, k_ref, vt_ref, gain_ref, bd64_ref, o_ref, ot_scr, *scratch,
                      tq, tk, out_scale):
    qi = pl.program_id(1)
    lam = lam_ref[0]

    def qt_of_head(h):
        return jnp.concatenate([qt_ref[0, h, 0], qt_ref[0, h, 1]], axis=1)

    _attention_sweep(qt_of_head, k_ref, vt_ref, qi, tq, tk, 2, scratch)
    for h in range(HEADS):
        o = _normalised_output(scratch[-1], h)
        ot_scr[h * HEAD_DIM:(h + 1) * HEAD_DIM, :] = o[:, 0:tq] - lam * o[:, tq:2 * tq]
    y = _segment_rms(ot_scr[...].T, gain_ref[...] * out_scale, bd64_ref[...], HEAD_DIM)
    o_ref[0] = y.astype(BF16)


def _fox_attn_kernel(qt_ref, k_ref, vt_ref, g_ref, o_ref, ot_scr, *scratch, tq, tk):
    qi = pl.program_id(1)
    _attention_sweep(lambda h: qt_ref[0, h], k_ref, vt_ref, qi, tq, tk, 1, scratch)
    for h in range(HEADS):
        ot_scr[h * HEAD_DIM:(h + 1) * HEAD_DIM, :] = _normalised_output(scratch[-1], h)
    o_ref[0] = (ot_scr[...].T * jax.nn.sigmoid(g_ref[0])).astype(BF16)


def _attn_scratch(tq, tk, ncomp):
    cols = ncomp * tq
    return [
        pltpu.VMEM((GROUP_WIDTH, tq), F32),
        pltpu.VMEM((2, HEADS, tk, cols), F32),
        pltpu.VMEM((2, HEADS, tk, cols), BF16),
        pltpu.VMEM((HEADS, 1, cols), F32),
        pltpu.VMEM((HEADS, 1, cols), F32),
        pltpu.VMEM((HEADS, VT_ROWS, cols), F32),
    ]


def _normalised_output(acc_scr, h):
    return acc_scr[h, 0:HEAD_DIM, :] / acc_scr[h, HEAD_DIM:HEAD_DIM + 1, :]


def _diff_attention(qt, k, vt, lam, gain, bd64, out_scale, tq, tk):
    batch, _, seq, _ = k.shape
    g = GROUP_WIDTH
    return pl.pallas_call(
        functools.partial(_diff_attn_kernel, tq=tq, tk=tk, out_scale=out_scale),
        grid=(batch, seq // tq),
        in_specs=[
            pl.BlockSpec(memory_space=pltpu.SMEM),
            pl.BlockSpec((1, HEADS, 2, KEY_WIDTH, tq), lambda b, i: (b, 0, 0, 0, i)),
            pl.BlockSpec((1, HEADS, seq, KEY_WIDTH), lambda b, i: (b, 0, 0, 0)),
            pl.BlockSpec((1, HEADS, VT_ROWS, seq), lambda b, i: (b, 0, 0, 0)),
            pl.BlockSpec((1, g), lambda b, i: (0, 0)),
            pl.BlockSpec((g, g), lambda b, i: (0, 0)),
        ],
        out_specs=pl.BlockSpec((1, tq, g), lambda b, i: (b, i, 0)),
        out_shape=jax.ShapeDtypeStruct((batch, seq, g), BF16),
        scratch_shapes=_attn_scratch(tq, tk, 2),
        compiler_params=_params(("parallel", "arbitrary")),
        name="diff_attention",
    )(lam, qt, k, vt, gain, bd64)


def _fox_attention(qt, k, vt, z, tq, tk):
    batch, _, seq, _ = k.shape
    g = GROUP_WIDTH
    return pl.pallas_call(
        functools.partial(_fox_attn_kernel, tq=tq, tk=tk),
        grid=(batch, seq // tq),
        in_specs=[
            pl.BlockSpec((1, HEADS, KEY_WIDTH, tq), lambda b, i: (b, 0, 0, i)),
            pl.BlockSpec((1, HEADS, seq, KEY_WIDTH), lambda b, i: (b, 0, 0, 0)),
            pl.BlockSpec((1, HEADS, VT_ROWS, seq), lambda b, i: (b, 0, 0, 0)),
            pl.BlockSpec((1, tq, g), lambda b, i: (b, i, Z_FG)),
        ],
        out_specs=pl.BlockSpec((1, tq, g), lambda b, i: (b, i, 0)),
        out_shape=jax.ShapeDtypeStruct((batch, seq, g), BF16),
        scratch_shapes=_attn_scratch(tq, tk, 1),
        compiler_params=_params(("parallel", "arbitrary")),
        name="fox_attention",
    )(qt, k, vt, z)


def _head_masks(rows):
    lane_head = lax.broadcasted_iota(jnp.int32, (rows, GROUP_WIDTH), 1) // HEAD_DIM
    return [lane_head == h for h in range(HEADS)]


def _stack_heads(x, masks):
    return jnp.concatenate([jnp.where(m, x, 0.0) for m in masks], axis=0).astype(BF16)


def _hgrn2_kernel(q_ref, f_ref, i_ref, g_ref, lb_ref, gain_ref, bd64_ref, tri_ref, mask_ref,
                  o_ref, st_ref, *, chunks):
    @pl.when(pl.program_id(1) == 0)
    def _():
        st_ref[...] = jnp.zeros_like(st_ref)

    lb = lb_ref[...]
    bd64 = bd64_ref[...]
    tri = tri_ref[...]
    sub = 8
    gw = GROUP_WIDTH
    groups = CHUNK // sub
    sub_idx = lax.broadcasted_iota(jnp.int32, (groups, sub, gw), 1)
    row = lax.broadcasted_iota(jnp.int32, (CHUNK, gw), 0)
    key_pos = lax.broadcasted_iota(jnp.int32, (CHUNK, gw), 1) % HEAD_DIM
    head_masks = _head_masks(CHUNK)
    stack_heads = functools.partial(_stack_heads, masks=head_masks)

    span = range(chunks)
    rows = [slice(c * CHUNK, (c + 1) * CHUNK) for c in span]
    q = [q_ref[0, r, :] for r in rows]
    v = [i_ref[0, r, :] for r in rows]
    kk, b = [], []
    for c in span:
        f = lb + (1.0 - lb) * jax.nn.sigmoid(f_ref[0, rows[c], :])
        kk.append(1.0 - f)
        hi, mid, lo = _split3(jnp.log(f) * LOG2E)
        b.append(jnp.dot(tri, hi, preferred_element_type=F32)
                 + jnp.dot(tri, mid, preferred_element_type=F32)
                 + jnp.dot(tri, lo, preferred_element_type=F32))

    o = []
    for c in span:
        q3, k3, b3, v3 = (t.reshape(groups, sub, gw) for t in (q[c], kk[c], b[c], v[c]))
        o3 = jnp.zeros((groups, sub, gw), F32)
        for lag in range(sub):
            if lag == 0:
                x = q3 * k3
                vd = v3
            else:
                kd = pltpu.roll(k3, lag, 1)
                bd = pltpu.roll(b3, lag, 1)
                vd = pltpu.roll(v3, lag, 1)
                x = jnp.where(sub_idx >= lag, q3 * kd * jnp.exp2(b3 - bd), 0.0)
            a = jnp.dot(x.reshape(CHUNK, gw).astype(BF16), bd64, preferred_element_type=F32)
            o3 = o3 + a.reshape(groups, sub, gw) * vd
        o.append(o3.reshape(CHUNK, gw))

    a_far = []
    for c in span:
        total = None
        for half in (sub, 2 * sub, 4 * sub):
            block = 2 * half
            ref_rows = jnp.concatenate(
                [jnp.broadcast_to(b[c][p + half - 1:p + half, :], (block, gw)) for p in range(0, CHUNK, block)],
                axis=0)
            upper = ((row // half) & 1) == 1
            e = jnp.exp2(jnp.where(upper, b[c] - ref_rows, ref_rows - b[c]))
            qs = jnp.where(upper, q[c] * e, 0.0).astype(BF16)
            ks = stack_heads(jnp.where(upper, 0.0, kk[c] * e))
            a_lvl = lax.dot_general(qs, ks, (((1,), (1,)), ((), ())), preferred_element_type=F32)
            if block < CHUNK:
                a_lvl = jnp.where(row // block == key_pos // block, a_lvl, 0.0)
            total = a_lvl if total is None else total + a_lvl
        a_far.append(total.astype(BF16))
    for c in span:
        o[c] = o[c] + jnp.dot(a_far[c], stack_heads(v[c]), preferred_element_type=F32)

    upd = []
    for c in span:
        kp = (kk[c] * jnp.exp2(b[c][CHUNK - 1:CHUNK, :] - b[c])).astype(BF16)
        upd.append(lax.dot_general(v[c].astype(BF16), kp, (((0,), (0,)), ((), ())),
                                   preferred_element_type=F32))
    st = st_ref[...]
    for c in span:
        qe = (q[c] * jnp.exp2(b[c])).astype(BF16)
        o[c] = o[c] + lax.dot_general(qe, st.astype(BF16), (((1,), (1,)), ((), ())),
                                      preferred_element_type=F32)
        st = mask_ref[...] * (st * jnp.exp2(b[c][CHUNK - 1:CHUNK, :]) + upd[c])
    st_ref[...] = st

    for c in span:
        y = _segment_rms(o[c], gain_ref[...], bd64, HEAD_DIM)
        g = g_ref[0, rows[c], :]
        o_ref[0, rows[c], :] = (y * (g * jax.nn.sigmoid(g))).astype(BF16)


def _hgrn2(z, lb, gain, bd64, tri, mask, tile):
    batch, seq, _ = z.shape
    g = GROUP_WIDTH

    def zspec(block):
        return pl.BlockSpec((1, tile, g), lambda b, i: (b, i, block))

    const = lambda b, i: (0, 0)
    return pl.pallas_call(
        functools.partial(_hgrn2_kernel, chunks=tile // CHUNK),
        grid=(batch, seq // tile),
        in_specs=[
            zspec(Z_HQ), zspec(Z_HF), zspec(Z_HI), zspec(Z_HG),
            pl.BlockSpec((1, g), const), pl.BlockSpec((1, g), const),
            pl.BlockSpec((g, g), const), pl.BlockSpec((CHUNK, CHUNK), const),
            pl.BlockSpec((g, g), const),
        ],
        out_specs=pl.BlockSpec((1, tile, g), lambda b, i: (b, i, 0)),
        out_shape=jax.ShapeDtypeStruct((batch, seq, g), BF16),
        scratch_shapes=[pltpu.VMEM((g, g), F32)],
        compiler_params=_params(("parallel", "arbitrary")),
        name="hgrn2",
    )(z, z, z, z, lb, gain, bd64, tri, mask)


def _mlstm_kernel(q_ref, k_ref, v_ref, og_ref, gcol_ref, grow_ref, conv_ref, expand_ref, bd64_ref,
                  mask_ref, y_ref, prev_scr, c_scr, n_scr, *, tile):
    @pl.when(pl.program_id(1) == 0)
    def _():
        prev_scr[...] = jnp.zeros_like(prev_scr)
        c_scr[...] = jnp.zeros_like(c_scr)
        n_scr[...] = jnp.zeros_like(n_scr)

    g = GROUP_WIDTH
    halo = prev_scr.shape[0]
    raw = jnp.concatenate([q_ref[0], k_ref[0]], axis=1)
    padded = jnp.concatenate([prev_scr[...], raw], axis=0)
    w = conv_ref[...]
    conv = w[CONV_WIDTH - 1:CONV_WIDTH, :] * raw
    for back in range(1, CONV_WIDTH):
        tap = CONV_WIDTH - 1 - back
        conv = conv + w[tap:tap + 1, :] * pltpu.roll(padded, back, 0)[halo:]
    prev_scr[...] = raw[tile - halo:]
    act = conv * jax.nn.sigmoid(conv)
    qc = act[:, :g] * HEAD_DIM ** -0.5
    kc = act[:, g:]
    v = v_ref[0]

    bd64 = bd64_ref[...]
    expand = expand_ref[...]
    chunks = tile // CHUNK
    span = range(chunks)
    rows = [slice(c * CHUNK, (c + 1) * CHUNK) for c in span]
    t_idx = lax.broadcasted_iota(jnp.int32, (CHUNK, g), 0)
    s_idx = lax.broadcasted_iota(jnp.int32, (CHUNK, g), 1) % HEAD_DIM
    causal = s_idx <= t_idx

    m_b, w_state_b, floor_b, w_key_b, g_b = [], [], [], [], []
    for c in span:
        wide = sum(lax.dot_general(part, expand, (((0,), (0,)), ((), ())), preferred_element_type=F32)
                   for part in _split3(gcol_ref[0, :, rows[c]]))
        for dst, col in ((m_b, COL_ML_M), (w_state_b, COL_ML_W_STATE), (floor_b, COL_ML_FLOOR),
                         (w_key_b, COL_ML_W_KEY)):
            dst.append(wide[:, col // HEADS * g:(col // HEADS + 1) * g])
        g_b.append(jnp.concatenate(
            [grow_ref[0, ROW_ML_G + h:ROW_ML_G + h + 1, rows[c]] for h in range(HEADS)], axis=1))

    qb = [qc[r].astype(BF16) for r in rows]
    head_masks = _head_masks(CHUNK)
    v_stack = [_stack_heads(v[r], head_masks) for r in rows]
    scores = [lax.dot_general(qb[c], _stack_heads(kc[rows[c]], head_masks), (((1,), (1,)), ((), ())),
                              preferred_element_type=F32) for c in span]
    num_intra, den_intra = [], []
    for c in span:
        s = scores[c] * jnp.exp(jnp.where(causal, g_b[c] - m_b[c], -jnp.inf))
        num_intra.append(jnp.dot(s.astype(BF16), v_stack[c], preferred_element_type=F32))
        den_intra.append(_segment_sum(s, bd64))
    c_upd, n_upd = [], []
    for c in span:
        kw = kc[rows[c]] * w_key_b[c]
        c_upd.append(mask_ref[...] * lax.dot_general(kw.astype(BF16), v[rows[c]].astype(BF16),
                                                     (((0,), (0,)), ((), ())), preferred_element_type=F32))
        n_upd.append(jnp.sum(kw, axis=0, keepdims=True))

    cm = c_scr[...]
    nv = n_scr[...]
    for c in span:
        inter = jnp.dot(qb[c], cm.astype(BF16), preferred_element_type=F32)
        q_dot_n = _segment_sum(qc[rows[c]] * nv, bd64)
        num = num_intra[c] + w_state_b[c] * inter
        den = den_intra[c] + w_state_b[c] * q_dot_n
        y = num / jnp.maximum(jnp.abs(den), floor_b[c])
        y_ref[0, rows[c], :] = (jax.nn.sigmoid(og_ref[0, rows[c], :]) * y).astype(BF16)
        decay = w_state_b[c][CHUNK - 1:CHUNK, :]
        cm = decay * cm + c_upd[c]
        nv = decay * nv + n_upd[c]
    c_scr[...] = cm
    n_scr[...] = nv


def _mlstm(z, gcol, grow, conv_w, bd64, mask, tile):
    batch, seq, _ = z.shape
    g = GROUP_WIDTH
    expand = jnp.asarray(np.repeat(np.eye(GATE_ROWS, dtype=np.float32), HEAD_DIM, axis=1), dtype=BF16)

    def zspec(block):
        return pl.BlockSpec((1, tile, g), lambda b, i: (b, i, block))

    const = lambda b, i: (0, 0)
    return pl.pallas_call(
        functools.partial(_mlstm_kernel, tile=tile),
        grid=(batch, seq // tile),
        in_specs=[
            zspec(Z_MQ), zspec(Z_MK), zspec(Z_MV), zspec(Z_MO),
            pl.BlockSpec((1, GATE_ROWS, tile), lambda b, i: (b, 0, i)),
            pl.BlockSpec((1, GATE_ROWS, tile), lambda b, i: (b, 0, i)),
            pl.BlockSpec((CONV_WIDTH, 2 * g), const),
            pl.BlockSpec((GATE_ROWS, GATE_ROWS * HEAD_DIM), const),
            pl.BlockSpec((g, g), const),
            pl.BlockSpec((g, g), const),
        ],
        out_specs=pl.BlockSpec((1, tile, g), lambda b, i: (b, i, 0)),
        out_shape=jax.ShapeDtypeStruct((batch, seq, g), BF16),
        scratch_shapes=[
            pltpu.VMEM((8, 2 * g), F32),
            pltpu.VMEM((g, g), F32),
            pltpu.VMEM((1, g), F32),
        ],
        compiler_params=_params(("parallel", "arbitrary")),
        name="mlstm",
    )(z, z, z, z, gcol, grow, conv_w, expand, bd64, mask)


def _tile_rows(v, reps):
    return jnp.tile(v.astype(F32), reps).reshape(1, -1)


def kernel(x, c, w_ada, b_ada, norm_mix_gain, norm_ff_gain, w_in, w_out, hg_lb_logits, hg_norm_gain,
           diff_qn_gain, diff_kn_gain, diff_lambda, diff_sub_gain, fox_qn_gain, fox_kn_gain, fox_f_bias,
           mlstm_conv, mlstm_i_bias, mlstm_f_bias, w_ff1, w_ff2):
    depth = w_in.shape[0]
    batch, seq, d = x.shape
    g, hd = GROUP_WIDTH, HEAD_DIM
    tm = min(512, seq)
    tq = min(512, seq)
    tk = 128
    tr = min(512, seq)
    tp = min(1024, seq)

    p_lb = jax.nn.softmax(hg_lb_logits.astype(F32), axis=0)
    lower_bounds = jnp.cumsum(p_lb, axis=0) - p_lb[0:1]
    bd32 = _block_diag_ones(g, DIFF_QK_DIM)
    bd64 = _block_diag_ones(g, hd)
    bd64_f32 = bd64.astype(F32)
    tri = jnp.asarray(np.tril(np.ones((CHUNK, CHUNK), np.float32)), dtype=BF16)
    slopes = jnp.asarray(2.0 ** (-8.0 * np.arange(1, HEADS + 1) / HEADS), F32)

    mod = _ada_modulation(c.astype(F32), w_ada, b_ada)
    mod5 = mod.reshape(depth, batch, 6, 1, d)

    for l in range(depth):
        w_perm = _permuted_in_weights(w_in[l])
        z = _input_projection(x, mod5, l, norm_mix_gain[l], w_perm, tm)

        gbias = jnp.zeros((GATE_ROWS,), F32)
        gbias = gbias.at[ROW_FOX_F:ROW_FOX_F + HEADS].set(fox_f_bias[l].astype(F32))
        gbias = gbias.at[ROW_ML_B:ROW_ML_B + HEADS].set(mlstm_f_bias[l].astype(F32))
        gbias = gbias.at[ROW_ML_I:ROW_ML_I + HEADS].set(mlstm_i_bias[l].astype(F32))
        prep_params = (
            _tile_rows(diff_qn_gain[l], 2 * HEADS), _tile_rows(diff_kn_gain[l], 2 * HEADS),
            _tile_rows(fox_qn_gain[l], HEADS), _tile_rows(fox_kn_gain[l], HEADS),
            gbias.reshape(GATE_ROWS, 1), bd32, bd64,
        )
        dqt, dk, dvt, fqt, fk, fvt, grow, mcol_t = _attention_prep(z, slopes, prep_params, tp)

        lam_init = 0.8 - 0.6 * math.exp(-0.3 * l)
        lv = diff_lambda[l].astype(F32)
        lam = jnp.exp(jnp.sum(lv[0] * lv[1])) - jnp.exp(jnp.sum(lv[2] * lv[3])) + lam_init

        y_a = _hgrn2(z, lower_bounds[l].reshape(1, g), _tile_rows(hg_norm_gain[l], HEADS),
                     bd64, tri, bd64_f32, tr)
        y_b = _diff_attention(dqt, dk, dvt, lam.reshape(1), _tile_rows(diff_sub_gain[l], HEADS),
                              bd64, 1.0 - lam_init, tq, tk)
        y_c = _fox_attention(fqt, fk, fvt, z, tq, tk)
        y_d = _mlstm(z, mcol_t, grow, mlstm_conv[l].astype(F32), bd64, bd64_f32, tr)

        x = _out_projection_mlp(x, (y_a, y_b, y_c, y_d), mod5, l, norm_ff_gain[l],
                                w_out[l].astype(BF16), w_ff1[l].astype(BF16), w_ff2[l].astype(BF16), tm)
    return x
```

```python
import functools
import math

import numpy as np
import jax
import jax.numpy as jnp
from jax import lax
from jax.experimental import pallas as pl
from jax.experimental.pallas import tpu as pltpu

F32 = jnp.float32
BF16 = jnp.bfloat16

D_MODEL = 1024
N_GROUPS = 4
GROUP_WIDTH = D_MODEL // N_GROUPS
HEADS = 4
HEAD_DIM = GROUP_WIDTH // HEADS
DIFF_QK_DIM = HEAD_DIM // 2
D_FF = 4 * D_MODEL
CHUNK = 64
CONV_WIDTH = 4
NORM_EPS = 1e-6
N_FULL_GROUPS = 15
GATE_LANES = 128
Z_COLS = N_FULL_GROUPS * GROUP_WIDTH + GATE_LANES
GATE_ROWS = 16
KEY_WIDTH = 128
VT_ROWS = HEAD_DIM + 16
LOG2E = math.log2(math.e)

(Z_HQ, Z_HF, Z_HI, Z_HG, Z_DQ, Z_DK, Z_DV, Z_FQ, Z_FK, Z_FV, Z_FG,
 Z_MQ, Z_MK, Z_MV, Z_MO) = range(N_FULL_GROUPS)
Z_GATE_BLOCK = N_FULL_GROUPS * GROUP_WIDTH // GATE_LANES
ROW_FOX_F, ROW_ML_B, ROW_ML_G, ROW_ML_CG = 0, 4, 8, 12
ROW_ML_I = ROW_ML_G
COL_ML_M, COL_ML_W_STATE, COL_ML_FLOOR, COL_ML_W_KEY = 0, 4, 8, 12

VMEM_LIMIT = 56 * 1024 * 1024


def _params(sem):
    return pltpu.CompilerParams(dimension_semantics=sem, vmem_limit_bytes=VMEM_LIMIT)


def _in_column_permutation():
    g, h = GROUP_WIDTH, HEADS
    widths = (g,) * 4 + (g,) * 3 + (g,) * 4 + (h,) + (g,) * 4 + (h,) * 2
    starts = np.concatenate([[0], np.cumsum(widths)[:-1]])
    (hq, hf, hi, hg, dq, dk, dv, fq, fk, fv, fg, ff, mq, mk, mv, mo, mi, mf) = range(18)
    order_full = [hq, hf, hi, hg, dq, dk, dv, fq, fk, fv, fg, mq, mk, mv, mo]
    cols = [np.arange(starts[i], starts[i] + widths[i]) for i in order_full]
    gates = np.full((GATE_LANES,), -1, np.int64)
    gates[ROW_FOX_F:ROW_FOX_F + h] = np.arange(starts[ff], starts[ff] + h)
    gates[ROW_ML_B:ROW_ML_B + h] = np.arange(starts[mf], starts[mf] + h)
    gates[ROW_ML_I:ROW_ML_I + h] = np.arange(starts[mi], starts[mi] + h)
    return np.concatenate(cols + [gates])


def _permuted_in_weights(w):
    perm = _in_column_permutation()
    taken = jnp.take(w, jnp.asarray(np.maximum(perm, 0), jnp.int32), axis=1)
    return jnp.where(jnp.asarray(perm >= 0)[None, :], taken, 0.0).astype(BF16)


def _split3(x):
    hi = x.astype(BF16)
    r1 = x - hi.astype(F32)
    mid = r1.astype(BF16)
    lo = (r1 - mid.astype(F32)).astype(BF16)
    return hi, mid, lo


def _split2(x):
    hi = x.astype(BF16)
    lo = (x - hi.astype(F32)).astype(BF16)
    return hi, lo


def _segment_sum(x, ones_bd):
    hi, lo = _split2(x)
    return (jnp.dot(hi, ones_bd, preferred_element_type=F32)
            + jnp.dot(lo, ones_bd, preferred_element_type=F32))


def _block_diag_ones(n, seg):
    idx = np.arange(n) // seg
    return jnp.asarray((idx[:, None] == idx[None, :]).astype(np.float32), dtype=BF16)


def _ada_kernel(c_ref, w_ref, b_ref, o_ref):
    c = c_ref[...]
    c_act = c * jax.nn.sigmoid(c)
    o_ref[0] = jnp.dot(c_act, w_ref[0], preferred_element_type=F32) + b_ref[0]


def _ada_modulation(c, w_ada, b_ada):
    depth, d, n = w_ada.shape
    batch = c.shape[0]
    bn = 1024
    return pl.pallas_call(
        _ada_kernel,
        grid=(depth, n // bn),
        in_specs=[
            pl.BlockSpec((batch, d), lambda l, j: (0, 0)),
            pl.BlockSpec((1, d, bn), lambda l, j: (l, 0, j)),
            pl.BlockSpec((1, 1, bn), lambda l, j: (l, 0, j)),
        ],
        out_specs=pl.BlockSpec((1, batch, bn), lambda l, j: (l, 0, j)),
        out_shape=jax.ShapeDtypeStruct((depth, batch, n), F32),
        compiler_params=_params(("parallel", "parallel")),
        name="ada_modulation",
    )(c, w_ada, b_ada.reshape(depth, 1, n))


def _modulated_norm(x, gain, shift, scale):
    ms = jnp.mean(x * x, axis=-1, keepdims=True)
    return x * lax.rsqrt(ms + NORM_EPS) * gain * (1.0 + scale) + shift


def _inproj_kernel(x_ref, gain_ref, shift_ref, scale_ref, w_ref, z_ref, *, col_chunk):
    h = _modulated_norm(x_ref[0], gain_ref[...], shift_ref[...], scale_ref[...]).astype(BF16)
    for j in range(0, Z_COLS, col_chunk):
        n = min(col_chunk, Z_COLS - j)
        z_ref[0, :, j:j + n] = jnp.dot(h, w_ref[:, j:j + n], preferred_element_type=F32)


def _mod_spec(layer, which):
    return pl.BlockSpec((None, None, None, 1, D_MODEL), lambda b, i: (layer, b, which, 0, 0))


def _input_projection(x, mod5, layer, gain, w_perm, tm):
    batch, seq, d = x.shape
    return pl.pallas_call(
        functools.partial(_inproj_kernel, col_chunk=512),
        grid=(batch, seq // tm),
        in_specs=[
            pl.BlockSpec((1, tm, d), lambda b, i: (b, i, 0)),
            pl.BlockSpec((1, d), lambda b, i: (0, 0)),
            _mod_spec(layer, 0),
            _mod_spec(layer, 1),
            pl.BlockSpec((d, Z_COLS), lambda b, i: (0, 0)),
        ],
        out_specs=pl.BlockSpec((1, tm, Z_COLS), lambda b, i: (b, i, 0)),
        out_shape=jax.ShapeDtypeStruct((batch, seq, Z_COLS), F32),
        compiler_params=_params(("parallel", "parallel")),
        name="input_projection",
    )(x, gain.reshape(1, d), mod5, mod5, w_perm)


def _ffn_kernel(x_ref, ya_ref, yb_ref, yc_ref, yd_ref, gate1_ref, gain_ref, shift_ref, scale_ref,
                gate2_ref, wo_ref, w1_ref, w2_ref, o_ref, *, ff_chunk):
    g = GROUP_WIDTH
    y = jnp.dot(ya_ref[0], wo_ref[0:g, :], preferred_element_type=F32)
    y += jnp.dot(yb_ref[0], wo_ref[g:2 * g, :], preferred_element_type=F32)
    y += jnp.dot(yc_ref[0], wo_ref[2 * g:3 * g, :], preferred_element_type=F32)
    y += jnp.dot(yd_ref[0], wo_ref[3 * g:4 * g, :], preferred_element_type=F32)
    x1 = x_ref[0] + gate1_ref[...] * y
    h = _modulated_norm(x1, gain_ref[...], shift_ref[...], scale_ref[...]).astype(BF16)
    acc = jnp.zeros_like(x1)
    for j in range(0, D_FF, ff_chunk):
        u = jnp.dot(h, w1_ref[:, j:j + ff_chunk], preferred_element_type=F32)
        u = jnp.square(jnp.maximum(u, 0.0)).astype(BF16)
        acc += jnp.dot(u, w2_ref[j:j + ff_chunk, :], preferred_element_type=F32)
    o_ref[0] = x1 + gate2_ref[...] * acc


def _out_projection_mlp(x, ys, mod5, layer, gain, w_out, w1, w2, tm):
    batch, seq, d = x.shape
    g = GROUP_WIDTH
    y_spec = pl.BlockSpec((1, tm, g), lambda b, i: (b, i, 0))
    const = lambda b, i: (0, 0)
    return pl.pallas_call(
        functools.partial(_ffn_kernel, ff_chunk=512),
        grid=(batch, seq // tm),
        in_specs=[
            pl.BlockSpec((1, tm, d), lambda b, i: (b, i, 0)),
            y_spec, y_spec, y_spec, y_spec,
            _mod_spec(layer, 2),
            pl.BlockSpec((1, d), const),
            _mod_spec(layer, 3),
            _mod_spec(layer, 4),
            _mod_spec(layer, 5),
            pl.BlockSpec((d, d), const, pipeline_mode=pl.Buffered(1)),
            pl.BlockSpec((d, D_FF), const, pipeline_mode=pl.Buffered(1)),
            pl.BlockSpec((D_FF, d), const, pipeline_mode=pl.Buffered(1)),
        ],
        out_specs=pl.BlockSpec((1, tm, d), lambda b, i: (b, i, 0)),
        out_shape=jax.ShapeDtypeStruct((batch, seq, d), F32),
        compiler_params=_params(("parallel", "parallel")),
        name="out_projection_mlp",
    )(x, *ys, mod5, gain.reshape(1, d), mod5, mod5, mod5, w_out, w1, w2)


def _log_sigmoid(t):
    return jnp.minimum(t, 0.0) - jnp.log1p(jnp.exp(-jnp.abs(t)))


def _segment_rms(x, gain, ones_bd, seg):
    ss = jnp.dot((x * x).astype(BF16), ones_bd, preferred_element_type=F32)
    return x * lax.rsqrt(ss * (1.0 / seg) + NORM_EPS) * gain


def _split3_f32(x):
    hi = x.astype(BF16).astype(F32)
    r1 = x - hi
    mid = r1.astype(BF16).astype(F32)
    lo = (r1 - mid).astype(BF16).astype(F32)
    return hi, mid, lo


def _prep_kernel(slopes_ref, dq_ref, dk_ref, dv_ref, fq_ref, fk_ref, fv_ref, zg_ref,
                 dqn_ref, dkn_ref, fqn_ref, fkn_ref, gbias_ref, bd32_ref, bd64_ref,
                 dqt_o, dk_o, dvt_o, fqt_o, fk_o, fvt_o, grow_o, mcol_o, carry_ref, mcarry_ref, *, tile):
    bd32 = bd32_ref[...]
    bd64 = bd64_ref[...]
    t0 = pl.program_id(1) * tile

    @pl.when(pl.program_id(1) == 0)
    def _():
        carry_ref[...] = jnp.zeros_like(carry_ref)
        mcarry_ref[...] = jnp.zeros_like(mcarry_ref)

    t = zg_ref[0].T[0:GATE_ROWS, :] + gbias_ref[...]
    row = lax.broadcasted_iota(jnp.int32, (GATE_ROWS, tile), 0)
    lane = lax.broadcasted_iota(jnp.int32, (GATE_ROWS, tile), 1)
    whole_seq = row < ROW_ML_B
    per_chunk = (row >= ROW_ML_B) & (row < ROW_ML_I)
    acc = jnp.where(whole_seq | per_chunk, _log_sigmoid(t), 0.0)
    lane_in_chunk = lane & (CHUNK - 1)
    shift = 1
    while shift < tile:
        ok = whole_seq & (lane >= shift)
        if shift < CHUNK:
            ok = ok | (per_chunk & (lane_in_chunk >= shift))
        acc = acc + jnp.where(ok, pltpu.roll(acc, shift, 1), 0.0)
        shift *= 2
    acc = acc + jnp.where(whole_seq, carry_ref[:, 0:1], 0.0)
    carry_ref[...] = jnp.broadcast_to(acc[:, tile - 1:tile], carry_ref.shape)
    g_all = t - pltpu.roll(acc, ROW_ML_G - ROW_ML_B, 0)
    run_max = g_all
    shift = 1
    while shift < CHUNK:
        run_max = jnp.maximum(run_max, jnp.where(lane_in_chunk >= shift, pltpu.roll(run_max, shift, 1), -jnp.inf))
        shift *= 2
    grow_o[0] = jnp.where(row < ROW_ML_G, acc,
                          jnp.where(row < ROW_ML_CG, g_all, pltpu.roll(run_max, ROW_ML_CG - ROW_ML_G, 0)))

    def over_chunk(x, op):
        step = 1
        while step < CHUNK:
            x = op(x, jnp.where(lane_in_chunk < CHUNK - step, pltpu.roll(x, tile - step, 1), x))
            step *= 2
        return x

    b_g = pltpu.roll(acc, ROW_ML_G - ROW_ML_B, 0)
    b_last = over_chunk(b_g, jnp.minimum)
    g_max = over_chunk(run_max, jnp.maximum)
    a_map, d_map = b_last, g_max + b_last
    step = CHUNK
    while step < tile:
        earlier = lane >= step
        d_map = jnp.where(earlier, jnp.maximum(pltpu.roll(d_map, step, 1) + a_map, d_map), d_map)
        a_map = jnp.where(earlier, pltpu.roll(a_map, step, 1) + a_map, a_map)
        step *= 2
    m_tile = mcarry_ref[:, 0:1]
    m_after = jnp.maximum(m_tile + a_map, d_map)
    m_before = jnp.where(lane >= CHUNK, pltpu.roll(m_after, CHUNK, 1), m_tile)
    mcarry_ref[...] = jnp.broadcast_to(m_after[:, tile - 1:tile], mcarry_ref.shape)
    big_m = jnp.maximum(m_before, run_max)
    w_state = jnp.exp(m_before - big_m)
    floor = jnp.exp(-(b_g + big_m))
    w_key = jnp.exp(g_all - jnp.maximum(m_before, g_max))
    assert COL_ML_FLOOR == ROW_ML_G
    mcol_o[0] = jnp.where(
        row < COL_ML_W_STATE, pltpu.roll(big_m, GATE_ROWS + COL_ML_M - ROW_ML_G, 0),
        jnp.where(row < COL_ML_FLOOR, pltpu.roll(w_state, GATE_ROWS + COL_ML_W_STATE - ROW_ML_G, 0),
                  jnp.where(row < COL_ML_W_KEY, floor, pltpu.roll(w_key, COL_ML_W_KEY - ROW_ML_G, 0))))

    hd = HEAD_DIM
    lane_pos = (t0 + lax.broadcasted_iota(jnp.int32, (1, tile), 1)).astype(F32)
    feat_row = lax.broadcasted_iota(jnp.int32, (hd, tile), 0)
    feat_lane = lax.broadcasted_iota(jnp.int32, (tile, KEY_WIDTH), 1)
    sub_row = lax.broadcasted_iota(jnp.int32, (8, tile), 0)
    pad_rows = jnp.zeros((KEY_WIDTH - hd - 8, tile), F32)

    def bias_features(bias_row, bias_first):
        hi, mid, lo = _split3_f32(bias_row)
        parts = jnp.where(sub_row % 3 == 0, hi, jnp.where(sub_row % 3 == 1, mid, lo))
        first, second = sub_row < 3, (sub_row >= 3) & (sub_row < 6)
        in_bias, in_ones = (first, second) if bias_first else (second, first)
        return jnp.where(in_bias, parts, jnp.where(in_ones, 1.0, 0.0))

    def query_aug(bias_row):
        return jnp.concatenate([bias_features(bias_row, False), pad_rows], axis=0)

    def key_aug(k_all, h, bias_row):
        blk = k_all[:, (h // 2) * KEY_WIDTH:(h // 2 + 1) * KEY_WIDTH]
        k_head = pltpu.roll(blk, hd, 1) if h % 2 else blk
        extra = jnp.concatenate([jnp.zeros((hd, tile), F32), bias_features(bias_row, True), pad_rows], axis=0)
        return jnp.where(feat_lane < hd, k_head, extra.T)

    dq_t = (_segment_rms(dq_ref[0], dqn_ref[...], bd32, DIFF_QK_DIM) * (DIFF_QK_DIM ** -0.5 * LOG2E)).T
    dk = _segment_rms(dk_ref[0], dkn_ref[...], bd32, DIFF_QK_DIM)
    dv_t = dv_ref[0].T
    fq_t = (_segment_rms(fq_ref[0], fqn_ref[...], bd64, hd) * (hd ** -0.5 * LOG2E)).T
    fk = _segment_rms(fk_ref[0], fkn_ref[...], bd64, hd)
    fv_t = fv_ref[0].T
    ones_rows = jnp.ones((VT_ROWS - hd, tile), F32)
    for h in range(HEADS):
        rows = slice(h * hd, (h + 1) * hd)
        slope = slopes_ref[h] * LOG2E
        aug = query_aug(-slope * lane_pos)
        for comp in range(2):
            in_comp = (feat_row >= comp * DIFF_QK_DIM) & (feat_row < (comp + 1) * DIFF_QK_DIM)
            q_comp = jnp.where(in_comp, dq_t[rows], 0.0)
            dqt_o[0, h, comp] = jnp.concatenate([q_comp, aug], axis=0).astype(BF16)
        dk_o[0, h] = key_aug(dk, h, slope * lane_pos).astype(BF16)
        dvt_o[0, h] = jnp.concatenate([dv_t[rows], ones_rows], axis=0).astype(BF16)
        f_row = acc[ROW_FOX_F + h:ROW_FOX_F + h + 1, :] * LOG2E
        fqt_o[0, h] = jnp.concatenate([fq_t[rows], query_aug(f_row)], axis=0).astype(BF16)
        fk_o[0, h] = key_aug(fk, h, -f_row).astype(BF16)
        fvt_o[0, h] = jnp.concatenate([fv_t[rows], ones_rows], axis=0).astype(BF16)


def _attention_prep(z, slopes, layer_params, tile):
    batch, seq, _ = z.shape
    g = GROUP_WIDTH
    dqn, dkn, fqn, fkn, gbias, bd32, bd64 = layer_params

    def zspec(block):
        return pl.BlockSpec((1, tile, g), lambda b, i: (b, i, block))

    const = lambda b, i: (0, 0)
    qt_spec = pl.BlockSpec((1, HEADS, KEY_WIDTH, tile), lambda b, i: (b, 0, 0, i))
    qt2_spec = pl.BlockSpec((1, HEADS, 2, KEY_WIDTH, tile), lambda b, i: (b, 0, 0, 0, i))
    k_spec = pl.BlockSpec((1, HEADS, tile, KEY_WIDTH), lambda b, i: (b, 0, i, 0))
    vt_spec = pl.BlockSpec((1, HEADS, VT_ROWS, tile), lambda b, i: (b, 0, 0, i))
    qt_shape = jax.ShapeDtypeStruct((batch, HEADS, KEY_WIDTH, seq), BF16)
    qt2_shape = jax.ShapeDtypeStruct((batch, HEADS, 2, KEY_WIDTH, seq), BF16)
    k_shape = jax.ShapeDtypeStruct((batch, HEADS, seq, KEY_WIDTH), BF16)
    vt_shape = jax.ShapeDtypeStruct((batch, HEADS, VT_ROWS, seq), BF16)
    return pl.pallas_call(
        functools.partial(_prep_kernel, tile=tile),
        grid=(batch, seq // tile),
        in_specs=[
            pl.BlockSpec(memory_space=pltpu.SMEM),
            zspec(Z_DQ), zspec(Z_DK), zspec(Z_DV), zspec(Z_FQ), zspec(Z_FK), zspec(Z_FV),
            pl.BlockSpec((1, tile, GATE_LANES), lambda b, i: (b, i, Z_GATE_BLOCK)),
            pl.BlockSpec((1, g), const), pl.BlockSpec((1, g), const),
            pl.BlockSpec((1, g), const), pl.BlockSpec((1, g), const),
            pl.BlockSpec((GATE_ROWS, 1), const),
            pl.BlockSpec((g, g), const), pl.BlockSpec((g, g), const),
        ],
        out_specs=[qt2_spec, k_spec, vt_spec, qt_spec, k_spec, vt_spec,
                   pl.BlockSpec((1, GATE_ROWS, tile), lambda b, i: (b, 0, i)),
                   pl.BlockSpec((1, GATE_ROWS, tile), lambda b, i: (b, 0, i))],
        out_shape=[qt2_shape, k_shape, vt_shape, qt_shape, k_shape, vt_shape,
                   jax.ShapeDtypeStruct((batch, GATE_ROWS, seq), F32),
                   jax.ShapeDtypeStruct((batch, GATE_ROWS, seq), F32)],
        scratch_shapes=[pltpu.VMEM((GATE_ROWS, GATE_LANES), F32), pltpu.VMEM((GATE_ROWS, GATE_LANES), F32)],
        compiler_params=_params(("parallel", "arbitrary")),
        name="attention_prep",
    )(slopes, z, z, z, z, z, z, z, dqn, dkn, fqn, fkn, gbias, bd32, bd64)


def _attention_sweep(qt_of_head, k_ref, vt_ref, qi, tq, tk, ncomp, scratch):
    cols = ncomp * tq
    assert tq % tk == 0
    s_scr, p_scr, m_scr, a_scr, acc_scr = scratch
    m_scr[...] = jnp.full(m_scr.shape, -jnp.inf, F32)
    a_scr[...] = jnp.ones(a_scr.shape, F32)
    acc_scr[...] = jnp.zeros(acc_scr.shape, F32)
    p_scr[1] = jnp.zeros(p_scr.shape[1:], BF16)
    per_block = tq // tk
    n_full = per_block * qi
    every = [(0, cols)]

    def visible(d):
        return [(c * tq + d * tk, (c + 1) * tq) for c in range(ncomp)]

    def logits(h, j, ranges):
        start = pl.multiple_of(j * tk, tk)
        k = k_ref[0, h, pl.ds(start, tk), :]
        qt = qt_of_head(h)
        return [jnp.dot(k, qt[:, lo:hi], preferred_element_type=F32) for lo, hi in ranges]

    def weighted_values(h, j, slot, ranges):
        start = pl.multiple_of(j * tk, tk)
        vt = vt_ref[0, h, :, pl.ds(start, tk)]
        return [jnp.dot(vt, p_scr[slot, h, :, lo:hi], preferred_element_type=F32) for lo, hi in ranges]

    def accumulate(pv, ranges):
        for h in range(HEADS):
            for (lo, hi), x in zip(ranges, pv[h]):
                acc_scr[h, :, lo:hi] = a_scr[h, :, lo:hi] * acc_scr[h, :, lo:hi] + x

    def stage(j, slot, prev_j, prev, cur, nxt, masked):
        pv = [weighted_values(h, prev_j, 1 - slot, prev) for h in range(HEADS)]
        s_next = [logits(h, j + 1, nxt) for h in range(HEADS)] if nxt else None
        alphas = []
        for h in range(HEADS):
            alphas.append([])
            for lo, hi in cur:
                s = s_scr[slot, h, :, lo:hi]
                if masked:
                    key = lax.broadcasted_iota(jnp.int32, (tk, hi - lo), 0)
                    query = (lax.broadcasted_iota(jnp.int32, (tk, hi - lo), 1) + lo) & (tq - 1)
                    s = jnp.where(key - query <= qi * tq - j * tk, s, -jnp.inf)
                m_prev = m_scr[h, :, lo:hi]
                m_new = jnp.maximum(m_prev, jnp.max(s, axis=0, keepdims=True))
                alphas[h].append(jnp.exp2(m_prev - m_new))
                p_scr[slot, h, :, lo:hi] = jnp.exp2(s - m_new).astype(BF16)
                m_scr[h, :, lo:hi] = m_new
        accumulate(pv, prev)
        for h in range(HEADS):
            for (lo, hi), alpha in zip(cur, alphas[h]):
                a_scr[h, :, lo:hi] = alpha
            if nxt:
                for (lo, hi), x in zip(nxt, s_next[h]):
                    s_scr[1 - slot, h, :, lo:hi] = x

    for h in range(HEADS):
        s_scr[0, h] = logits(h, 0, every)[0]

    def slot_of(j, d):
        return d % 2 if per_block % 2 == 0 else j & 1

    def full_block(i, carry):
        for d in range(per_block):
            j = i * per_block + d
            stage(j, slot_of(j, d), jnp.maximum(j - 1, 0), every, every, every, False)
        return carry

    lax.fori_loop(0, qi, full_block, 0)
    for d in range(per_block):
        j = n_full + d
        stage(j, slot_of(j, d), jnp.maximum(j - 1, 0), visible(d - 1) if d else every, visible(d),
              visible(d + 1) if d + 1 < per_block else None, True)
    j_last = n_full + per_block - 1
    last = visible(per_block - 1)
    accumulate([weighted_values(h, j_last, slot_of(j_last, per_block - 1), last) for h in range(HEADS)], last)


def _diff_attn_kernel(lam_ref, qt_ref, k_ref, vt_ref, gain_ref, bd64_ref, o_ref, ot_scr, *scratch,
                      tq, tk, out_scale):
    qi = pl.program_id(1)
    lam = lam_ref[0]

    def qt_of_head(h):
        return jnp.concatenate([qt_ref[0, h, 0], qt_ref[0, h, 1]], axis=1)

    _attention_sweep(qt_of_head, k_ref, vt_ref, qi, tq, tk, 2, scratch)
    for h in range(HEADS):
        o = _normalised_output(scratch[-1], h)
        ot_scr[h * HEAD_DIM:(h + 1) * HEAD_DIM, :] = o[:, 0:tq] - lam * o[:, tq:2 * tq]
    y = _segment_rms(ot_scr[...].T, gain_ref[...] * out_scale, bd64_ref[...], HEAD_DIM)
    o_ref[0] = y.astype(BF16)


def _fox_attn_kernel(qt_ref, k_ref, vt_ref, g_ref, o_ref, ot_scr, *scratch, tq, tk):
    qi = pl.program_id(1)
    _attention_sweep(lambda h: qt_ref[0, h], k_ref, vt_ref, qi, tq, tk, 1, scratch)
    for h in range(HEADS):
        ot_scr[h * HEAD_DIM:(h + 1) * HEAD_DIM, :] = _normalised_output(scratch[-1], h)
    o_ref[0] = (ot_scr[...].T * jax.nn.sigmoid(g_ref[0])).astype(BF16)


def _attn_scratch(tq, tk, ncomp):
    cols = ncomp * tq
    return [
        pltpu.VMEM((GROUP_WIDTH, tq), F32),
        pltpu.VMEM((2, HEADS, tk, cols), F32),
        pltpu.VMEM((2, HEADS, tk, cols), BF16),
        pltpu.VMEM((HEADS, 1, cols), F32),
        pltpu.VMEM((HEADS, 1, cols), F32),
        pltpu.VMEM((HEADS, VT_ROWS, cols), F32),
    ]


def _normalised_output(acc_scr, h):
    return acc_scr[h, 0:HEAD_DIM, :] / acc_scr[h, HEAD_DIM:HEAD_DIM + 1, :]


def _diff_attention(qt, k, vt, lam, gain, bd64, out_scale, tq, tk):
    batch, _, seq, _ = k.shape
    g = GROUP_WIDTH
    return pl.pallas_call(
        functools.partial(_diff_attn_kernel, tq=tq, tk=tk, out_scale=out_scale),
        grid=(batch, seq // tq),
        in_specs=[
            pl.BlockSpec(memory_space=pltpu.SMEM),
            pl.BlockSpec((1, HEADS, 2, KEY_WIDTH, tq), lambda b, i: (b, 0, 0, 0, i)),
            pl.BlockSpec((1, HEADS, seq, KEY_WIDTH), lambda b, i: (b, 0, 0, 0)),
            pl.BlockSpec((1, HEADS, VT_ROWS, seq), lambda b, i: (b, 0, 0, 0)),
            pl.BlockSpec((1, g), lambda b, i: (0, 0)),
            pl.BlockSpec((g, g), lambda b, i: (0, 0)),
        ],
        out_specs=pl.BlockSpec((1, tq, g), lambda b, i: (b, i, 0)),
        out_shape=jax.ShapeDtypeStruct((batch, seq, g), BF16),
        scratch_shapes=_attn_scratch(tq, tk, 2),
        compiler_params=_params(("parallel", "arbitrary")),
        name="diff_attention",
    )(lam, qt, k, vt, gain, bd64)


def _fox_attention(qt, k, vt, z, tq, tk):
    batch, _, seq, _ = k.shape
    g = GROUP_WIDTH
    return pl.pallas_call(
        functools.partial(_fox_attn_kernel, tq=tq, tk=tk),
        grid=(batch, seq // tq),
        in_specs=[
            pl.BlockSpec((1, HEADS, KEY_WIDTH, tq), lambda b, i: (b, 0, 0, i)),
            pl.BlockSpec((1, HEADS, seq, KEY_WIDTH), lambda b, i: (b, 0, 0, 0)),
            pl.BlockSpec((1, HEADS, VT_ROWS, seq), lambda b, i: (b, 0, 0, 0)),
            pl.BlockSpec((1, tq, g), lambda b, i: (b, i, Z_FG)),
        ],
        out_specs=pl.BlockSpec((1, tq, g), lambda b, i: (b, i, 0)),
        out_shape=jax.ShapeDtypeStruct((batch, seq, g), BF16),
        scratch_shapes=_attn_scratch(tq, tk, 1),
        compiler_params=_params(("parallel", "arbitrary")),
        name="fox_attention",
    )(qt, k, vt, z)


def _head_masks(rows):
    lane_head = lax.broadcasted_iota(jnp.int32, (rows, GROUP_WIDTH), 1) // HEAD_DIM
    return [lane_head == h for h in range(HEADS)]


def _stack_heads(x, masks):
    return jnp.concatenate([jnp.where(m, x, 0.0) for m in masks], axis=0).astype(BF16)


def _hgrn2_kernel(q_ref, f_ref, i_ref, g_ref, lb_ref, gain_ref, bd64_ref, tri_ref, mask_ref,
                  o_ref, st_ref, *, chunks):
    @pl.when(pl.program_id(1) == 0)
    def _():
        st_ref[...] = jnp.zeros_like(st_ref)

    lb = lb_ref[...]
    bd64 = bd64_ref[...]
    tri = tri_ref[...]
    sub = 8
    gw = GROUP_WIDTH
    groups = CHUNK // sub
    sub_idx = lax.broadcasted_iota(jnp.int32, (groups, sub, gw), 1)
    row = lax.broadcasted_iota(jnp.int32, (CHUNK, gw), 0)
    key_pos = lax.broadcasted_iota(jnp.int32, (CHUNK, gw), 1) % HEAD_DIM
    head_masks = _head_masks(CHUNK)
    stack_heads = functools.partial(_stack_heads, masks=head_masks)

    span = range(chunks)
    rows = [slice(c * CHUNK, (c + 1) * CHUNK) for c in span]
    q = [q_ref[0, r, :] for r in rows]
    v = [i_ref[0, r, :] for r in rows]
    kk, b = [], []
    for c in span:
        f = lb + (1.0 - lb) * jax.nn.sigmoid(f_ref[0, rows[c], :])
        kk.append(1.0 - f)
        hi, mid, lo = _split3(jnp.log(f) * LOG2E)
        b.append(jnp.dot(tri, hi, preferred_element_type=F32)
                 + jnp.dot(tri, mid, preferred_element_type=F32)
                 + jnp.dot(tri, lo, preferred_element_type=F32))

    o = []
    for c in span:
        q3, k3, b3, v3 = (t.reshape(groups, sub, gw) for t in (q[c], kk[c], b[c], v[c]))
        o3 = jnp.zeros((groups, sub, gw), F32)
        for lag in range(sub):
            if lag == 0:
                x = q3 * k3
                vd = v3
            else:
                kd = pltpu.roll(k3, lag, 1)
                bd = pltpu.roll(b3, lag, 1)
                vd = pltpu.roll(v3, lag, 1)
                x = jnp.where(sub_idx >= lag, q3 * kd * jnp.exp2(b3 - bd), 0.0)
            a = jnp.dot(x.reshape(CHUNK, gw).astype(BF16), bd64, preferred_element_type=F32)
            o3 = o3 + a.reshape(groups, sub, gw) * vd
        o.append(o3.reshape(CHUNK, gw))

    a_far = []
    for c in span:
        total = None
        for half in (sub, 2 * sub, 4 * sub):
            block = 2 * half
            ref_rows = jnp.concatenate(
                [jnp.broadcast_to(b[c][p + half - 1:p + half, :], (block, gw)) for p in range(0, CHUNK, block)],
                axis=0)
            upper = ((row // half) & 1) == 1
            e = jnp.exp2(jnp.where(upper, b[c] - ref_rows, ref_rows - b[c]))
            qs = jnp.where(upper, q[c] * e, 0.0).astype(BF16)
            ks = stack_heads(jnp.where(upper, 0.0, kk[c] * e))
            a_lvl = lax.dot_general(qs, ks, (((1,), (1,)), ((), ())), preferred_element_type=F32)
            if block < CHUNK:
                a_lvl = jnp.where(row // block == key_pos // block, a_lvl, 0.0)
            total = a_lvl if total is None else total + a_lvl
        a_far.append(total.astype(BF16))
    for c in span:
        o[c] = o[c] + jnp.dot(a_far[c], stack_heads(v[c]), preferred_element_type=F32)

    upd = []
    for c in span:
        kp = (kk[c] * jnp.exp2(b[c][CHUNK - 1:CHUNK, :] - b[c])).astype(BF16)
        upd.append(lax.dot_general(v[c].astype(BF16), kp, (((0,), (0,)), ((), ())),
                                   preferred_element_type=F32))
    st = st_ref[...]
    for c in span:
        qe = (q[c] * jnp.exp2(b[c])).astype(BF16)
        o[c] = o[c] + lax.dot_general(qe, st.astype(BF16), (((1,), (1,)), ((), ())),
                                      preferred_element_type=F32)
        st = mask_ref[...] * (st * jnp.exp2(b[c][CHUNK - 1:CHUNK, :]) + upd[c])
    st_ref[...] = st

    for c in span:
        y = _segment_rms(o[c], gain_ref[...], bd64, HEAD_DIM)
        g = g_ref[0, rows[c], :]
        o_ref[0, rows[c], :] = (y * (g * jax.nn.sigmoid(g))).astype(BF16)


def _hgrn2(z, lb, gain, bd64, tri, mask, tile):
    batch, seq, _ = z.shape
    g = GROUP_WIDTH

    def zspec(block):
        return pl.BlockSpec((1, tile, g), lambda b, i: (b, i, block))

    const = lambda b, i: (0, 0)
    return pl.pallas_call(
        functools.partial(_hgrn2_kernel, chunks=tile // CHUNK),
        grid=(batch, seq // tile),
        in_specs=[
            zspec(Z_HQ), zspec(Z_HF), zspec(Z_HI), zspec(Z_HG),
            pl.BlockSpec((1, g), const), pl.BlockSpec((1, g), const),
            pl.BlockSpec((g, g), const), pl.BlockSpec((CHUNK, CHUNK), const),
            pl.BlockSpec((g, g), const),
        ],
        out_specs=pl.BlockSpec((1, tile, g), lambda b, i: (b, i, 0)),
        out_shape=jax.ShapeDtypeStruct((batch, seq, g), BF16),
        scratch_shapes=[pltpu.VMEM((g, g), F32)],
        compiler_params=_params(("parallel", "arbitrary")),
        name="hgrn2",
    )(z, z, z, z, lb, gain, bd64, tri, mask)


def _mlstm_kernel(q_ref, k_ref, v_ref, og_ref, gcol_ref, grow_ref, conv_ref, expand_ref, bd64_ref,
                  mask_ref, y_ref, prev_scr, c_scr, n_scr, *, tile):
    @pl.when(pl.program_id(1) == 0)
    def _():
        prev_scr[...] = jnp.zeros_like(prev_scr)
        c_scr[...] = jnp.zeros_like(c_scr)
        n_scr[...] = jnp.zeros_like(n_scr)

    g = GROUP_WIDTH
    halo = prev_scr.shape[0]
    raw = jnp.concatenate([q_ref[0], k_ref[0]], axis=1)
    padded = jnp.concatenate([prev_scr[...], raw], axis=0)
    w = conv_ref[...]
    conv = w[CONV_WIDTH - 1:CONV_WIDTH, :] * raw
    for back in range(1, CONV_WIDTH):
        tap = CONV_WIDTH - 1 - back
        conv = conv + w[tap:tap + 1, :] * pltpu.roll(padded, back, 0)[halo:]
    prev_scr[...] = raw[tile - halo:]
    act = conv * jax.nn.sigmoid(conv)
    qc = act[:, :g] * HEAD_DIM ** -0.5
    kc = act[:, g:]
    v = v_ref[0]

    bd64 = bd64_ref[...]
    expand = expand_ref[...]
    chunks = tile // CHUNK
    span = range(chunks)
    rows = [slice(c * CHUNK, (c + 1) * CHUNK) for c in span]
    t_idx = lax.broadcasted_iota(jnp.int32, (CHUNK, g), 0)
    s_idx = lax.broadcasted_iota(jnp.int32, (CHUNK, g), 1) % HEAD_DIM
    causal = s_idx <= t_idx

    m_b, w_state_b, floor_b, w_key_b, g_b = [], [], [], [], []
    for c in span:
        wide = sum(lax.dot_general(part, expand, (((0,), (0,)), ((), ())), preferred_element_type=F32)
                   for part in _split3(gcol_ref[0, :, rows[c]]))
        for dst, col in ((m_b, COL_ML_M), (w_state_b, COL_ML_W_STATE), (floor_b, COL_ML_FLOOR),
                         (w_key_b, COL_ML_W_KEY)):
            dst.append(wide[:, col // HEADS * g:(col // HEADS + 1) * g])
        g_b.append(jnp.concatenate(
            [grow_ref[0, ROW_ML_G + h:ROW_ML_G + h + 1, rows[c]] for h in range(HEADS)], axis=1))

    qb = [qc[r].astype(BF16) for r in rows]
    head_masks = _head_masks(CHUNK)
    v_stack = [_stack_heads(v[r], head_masks) for r in rows]
    scores = [lax.dot_general(qb[c], _stack_heads(kc[rows[c]], head_masks), (((1,), (1,)), ((), ())),
                              preferred_element_type=F32) for c in span]
    num_intra, den_intra = [], []
    for c in span:
        s = scores[c] * jnp.exp(jnp.where(causal, g_b[c] - m_b[c], -jnp.inf))
        num_intra.append(jnp.dot(s.astype(BF16), v_stack[c], preferred_element_type=F32))
        den_intra.append(_segment_sum(s, bd64))
    c_upd, n_upd = [], []
    for c in span:
        kw = kc[rows[c]] * w_key_b[c]
        c_upd.append(mask_ref[...] * lax.dot_general(kw.astype(BF16), v[rows[c]].astype(BF16),
                                                     (((0,), (0,)), ((), ())), preferred_element_type=F32))
        n_upd.append(jnp.sum(kw, axis=0, keepdims=True))

    cm = c_scr[...]
    nv = n_scr[...]
    for c in span:
        inter = jnp.dot(qb[c], cm.astype(BF16), preferred_element_type=F32)
        q_dot_n = _segment_sum(qc[rows[c]] * nv, bd64)
        num = num_intra[c] + w_state_b[c] * inter
        den = den_intra[c] + w_state_b[c] * q_dot_n
        y = num / jnp.maximum(jnp.abs(den), floor_b[c])
        y_ref[0, rows[c], :] = (jax.nn.sigmoid(og_ref[0, rows[c], :]) * y).astype(BF16)
        decay = w_state_b[c][CHUNK - 1:CHUNK, :]
        cm = decay * cm + c_upd[c]
        nv = decay * nv + n_upd[c]
    c_scr[...] = cm
    n_scr[...] = nv


def _mlstm(z, gcol, grow, conv_w, bd64, mask, tile):
    batch, seq, _ = z.shape
    g = GROUP_WIDTH
    expand = jnp.asarray(np.repeat(np.eye(GATE_ROWS, dtype=np.float32), HEAD_DIM, axis=1), dtype=BF16)

    def zspec(block):
        return pl.BlockSpec((1, tile, g), lambda b, i: (b, i, block))

    const = lambda b, i: (0, 0)
    return pl.pallas_call(
        functools.partial(_mlstm_kernel, tile=tile),
        grid=(batch, seq // tile),
        in_specs=[
            zspec(Z_MQ), zspec(Z_MK), zspec(Z_MV), zspec(Z_MO),
            pl.BlockSpec((1, GATE_ROWS, tile), lambda b, i: (b, 0, i)),
            pl.BlockSpec((1, GATE_ROWS, tile), lambda b, i: (b, 0, i)),
            pl.BlockSpec((CONV_WIDTH, 2 * g), const),
            pl.BlockSpec((GATE_ROWS, GATE_ROWS * HEAD_DIM), const),
            pl.BlockSpec((g, g), const),
            pl.BlockSpec((g, g), const),
        ],
        out_specs=pl.BlockSpec((1, tile, g), lambda b, i: (b, i, 0)),
        out_shape=jax.ShapeDtypeStruct((batch, seq, g), BF16),
        scratch_shapes=[
            pltpu.VMEM((8, 2 * g), F32),
            pltpu.VMEM((g, g), F32),
            pltpu.VMEM((1, g), F32),
        ],
        compiler_params=_params(("parallel", "arbitrary")),
        name="mlstm",
    )(z, z, z, z, gcol, grow, conv_w, expand, bd64, mask)


def _tile_rows(v, reps):
    return jnp.tile(v.astype(F32), reps).reshape(1, -1)


def kernel(x, c, w_ada, b_ada, norm_mix_gain, norm_ff_gain, w_in, w_out, hg_lb_logits, hg_norm_gain,
           diff_qn_gain, diff_kn_gain, diff_lambda, diff_sub_gain, fox_qn_gain, fox_kn_gain, fox_f_bias,
           mlstm_conv, mlstm_i_bias, mlstm_f_bias, w_ff1, w_ff2):
    depth = w_in.shape[0]
    batch, seq, d = x.shape
    g, hd = GROUP_WIDTH, HEAD_DIM
    tm = min(512, seq)
    tq = min(512, seq)
    tk = 128
    tr = min(512, seq)
    tp = min(1024, seq)

    p_lb = jax.nn.softmax(hg_lb_logits.astype(F32), axis=0)
    lower_bounds = jnp.cumsum(p_lb, axis=0) - p_lb[0:1]
    bd32 = _block_diag_ones(g, DIFF_QK_DIM)
    bd64 = _block_diag_ones(g, hd)
    bd64_f32 = bd64.astype(F32)
    tri = jnp.asarray(np.tril(np.ones((CHUNK, CHUNK), np.float32)), dtype=BF16)
    slopes = jnp.asarray(2.0 ** (-8.0 * np.arange(1, HEADS + 1) / HEADS), F32)

    mod = _ada_modulation(c.astype(F32), w_ada, b_ada)
    mod5 = mod.reshape(depth, batch, 6, 1, d)

    for l in range(depth):
        w_perm = _permuted_in_weights(w_in[l])
        z = _input_projection(x, mod5, l, norm_mix_gain[l], w_perm, tm)

        gbias = jnp.zeros((GATE_ROWS,), F32)
        gbias = gbias.at[ROW_FOX_F:ROW_FOX_F + HEADS].set(fox_f_bias[l].astype(F32))
        gbias = gbias.at[ROW_ML_B:ROW_ML_B + HEADS].set(mlstm_f_bias[l].astype(F32))
        gbias = gbias.at[ROW_ML_I:ROW_ML_I + HEADS].set(mlstm_i_bias[l].astype(F32))
        prep_params = (
            _tile_rows(diff_qn_gain[l], 2 * HEADS), _tile_rows(diff_kn_gain[l], 2 * HEADS),
            _tile_rows(fox_qn_gain[l], HEADS), _tile_rows(fox_kn_gain[l], HEADS),
            gbias.reshape(GATE_ROWS, 1), bd32, bd64,
        )
        dqt, dk, dvt, fqt, fk, fvt, grow, mcol_t = _attention_prep(z, slopes, prep_params, tp)

        lam_init = 0.8 - 0.6 * math.exp(-0.3 * l)
        lv = diff_lambda[l].astype(F32)
        lam = jnp.exp(jnp.sum(lv[0] * lv[1])) - jnp.exp(jnp.sum(lv[2] * lv[3])) + lam_init

        y_a = _hgrn2(z, lower_bounds[l].reshape(1, g), _tile_rows(hg_norm_gain[l], HEADS),
                     bd64, tri, bd64_f32, tr)
        y_b = _diff_attention(dqt, dk, dvt, lam.reshape(1), _tile_rows(diff_sub_gain[l], HEADS),
                              bd64, 1.0 - lam_init, tq, tk)
        y_c = _fox_attention(fqt, fk, fvt, z, tq, tk)
        y_d = _mlstm(z, mcol_t, grow, mlstm_conv[l].astype(F32), bd64, bd64_f32, tr)

        x = _out_projection_mlp(x, (y_a, y_b, y_c, y_d), mod5, l, norm_ff_gain[l],
                                w_out[l].astype(BF16), w_ff1[l].astype(BF16), w_ff2[l].astype(BF16), tm)
    return x
```

```python
import functools
import math

import numpy as np
import jax
import jax.numpy as jnp
from jax import lax
from jax.experimental import pallas as pl
from jax.experimental.pallas import tpu as pltpu

F32 = jnp.float32
BF16 = jnp.bfloat16

D_MODEL = 1024
N_GROUPS = 4
GROUP_WIDTH = D_MODEL // N_GROUPS
HEADS = 4
HEAD_DIM = GROUP_WIDTH // HEADS
DIFF_QK_DIM = HEAD_DIM // 2
D_FF = 4 * D_MODEL
CHUNK = 64
CONV_WIDTH = 4
NORM_EPS = 1e-6
N_FULL_GROUPS = 15
GATE_LANES = 128
Z_COLS = N_FULL_GROUPS * GROUP_WIDTH + GATE_LANES
GATE_ROWS = 16
KEY_WIDTH = 128
VT_ROWS = HEAD_DIM + 16
LOG2E = math.log2(math.e)

(Z_HQ, Z_HF, Z_HI, Z_HG, Z_DQ, Z_DK, Z_DV, Z_FQ, Z_FK, Z_FV, Z_FG,
 Z_MQ, Z_MK, Z_MV, Z_MO) = range(N_FULL_GROUPS)
Z_GATE_BLOCK = N_FULL_GROUPS * GROUP_WIDTH // GATE_LANES
ROW_FOX_F, ROW_ML_B, ROW_ML_G, ROW_ML_CG = 0, 4, 8, 12
ROW_ML_I = ROW_ML_G
COL_ML_M, COL_ML_W_STATE, COL_ML_FLOOR, COL_ML_W_KEY = 0, 4, 8, 12

VMEM_LIMIT = 56 * 1024 * 1024


def _params(sem):
    return pltpu.CompilerParams(dimension_semantics=sem, vmem_limit_bytes=VMEM_LIMIT)


def _in_column_permutation():
    g, h = GROUP_WIDTH, HEADS
    widths = (g,) * 4 + (g,) * 3 + (g,) * 4 + (h,) + (g,) * 4 + (h,) * 2
    starts = np.concatenate([[0], np.cumsum(widths)[:-1]])
    (hq, hf, hi, hg, dq, dk, dv, fq, fk, fv, fg, ff, mq, mk, mv, mo, mi, mf) = range(18)
    order_full = [hq, hf, hi, hg, dq, dk, dv, fq, fk, fv, fg, mq, mk, mv, mo]
    cols = [np.arange(starts[i], starts[i] + widths[i]) for i in order_full]
    gates = np.full((GATE_LANES,), -1, np.int64)
    gates[ROW_FOX_F:ROW_FOX_F + h] = np.arange(starts[ff], starts[ff] + h)
    gates[ROW_ML_B:ROW_ML_B + h] = np.arange(starts[mf], starts[mf] + h)
    gates[ROW_ML_I:ROW_ML_I + h] = np.arange(starts[mi], starts[mi] + h)
    return np.concatenate(cols + [gates])


def _permuted_in_weights(w):
    perm = _in_column_permutation()
    taken = jnp.take(w, jnp.asarray(np.maximum(perm, 0), jnp.int32), axis=1)
    return jnp.where(jnp.asarray(perm >= 0)[None, :], taken, 0.0).astype(BF16)


def _split3(x):
    hi = x.astype(BF16)
    r1 = x - hi.astype(F32)
    mid = r1.astype(BF16)
    lo = (r1 - mid.astype(F32)).astype(BF16)
    return hi, mid, lo


def _split2(x):
    hi = x.astype(BF16)
    lo = (x - hi.astype(F32)).astype(BF16)
    return hi, lo


def _segment_sum(x, ones_bd):
    hi, lo = _split2(x)
    return (jnp.dot(hi, ones_bd, preferred_element_type=F32)
            + jnp.dot(lo, ones_bd, preferred_element_type=F32))


def _block_diag_ones(n, seg):
    idx = np.arange(n) // seg
    return jnp.asarray((idx[:, None] == idx[None, :]).astype(np.float32), dtype=BF16)


def _ada_kernel(c_ref, w_ref, b_ref, o_ref):
    c = c_ref[...]
    c_act = c * jax.nn.sigmoid(c)
    o_ref[0] = jnp.dot(c_act, w_ref[0], preferred_element_type=F32) + b_ref[0]


def _ada_modulation(c, w_ada, b_ada):
    depth, d, n = w_ada.shape
    batch = c.shape[0]
    bn = 1024
    return pl.pallas_call(
        _ada_kernel,
        grid=(depth, n // bn),
        in_specs=[
            pl.BlockSpec((batch, d), lambda l, j: (0, 0)),
            pl.BlockSpec((1, d, bn), lambda l, j: (l, 0, j)),
            pl.BlockSpec((1, 1, bn), lambda l, j: (l, 0, j)),
        ],
        out_specs=pl.BlockSpec((1, batch, bn), lambda l, j: (l, 0, j)),
        out_shape=jax.ShapeDtypeStruct((depth, batch, n), F32),
        compiler_params=_params(("parallel", "parallel")),
        name="ada_modulation",
    )(c, w_ada, b_ada.reshape(depth, 1, n))


def _modulated_norm(x, gain, shift, scale):
    ms = jnp.mean(x * x, axis=-1, keepdims=True)
    return x * lax.rsqrt(ms + NORM_EPS) * gain * (1.0 + scale) + shift


def _inproj_kernel(x_ref, gain_ref, shift_ref, scale_ref, w_ref, z_ref, *, col_chunk):
    h = _modulated_norm(x_ref[0], gain_ref[...], shift_ref[...], scale_ref[...]).astype(BF16)
    for j in range(0, Z_COLS, col_chunk):
        n = min(col_chunk, Z_COLS - j)
        z_ref[0, :, j:j + n] = jnp.dot(h, w_ref[:, j:j + n], preferred_element_type=F32)


def _mod_spec(layer, which):
    return pl.BlockSpec((None, None, None, 1, D_MODEL), lambda b, i: (layer, b, which, 0, 0))


def _input_projection(x, mod5, layer, gain, w_perm, tm):
    batch, seq, d = x.shape
    return pl.pallas_call(
        functools.partial(_inproj_kernel, col_chunk=512),
        grid=(batch, seq // tm),
        in_specs=[
            pl.BlockSpec((1, tm, d), lambda b, i: (b, i, 0)),
            pl.BlockSpec((1, d), lambda b, i: (0, 0)),
            _mod_spec(layer, 0),
            _mod_spec(layer, 1),
            pl.BlockSpec((d, Z_COLS), lambda b, i: (0, 0)),
        ],
        out_specs=pl.BlockSpec((1, tm, Z_COLS), lambda b, i: (b, i, 0)),
        out_shape=jax.ShapeDtypeStruct((batch, seq, Z_COLS), F32),
        compiler_params=_params(("parallel", "parallel")),
        name="input_projection",
    )(x, gain.reshape(1, d), mod5, mod5, w_perm)


def _ffn_kernel(x_ref, ya_ref, yb_ref, yc_ref, yd_ref, gate1_ref, gain_ref, shift_ref, scale_ref,
                gate2_ref, wo_ref, w1_ref, w2_ref, o_ref, *, ff_chunk):
    g = GROUP_WIDTH
    y = jnp.dot(ya_ref[0], wo_ref[0:g, :], preferred_element_type=F32)
    y += jnp.dot(yb_ref[0], wo_ref[g:2 * g, :], preferred_element_type=F32)
    y += jnp.dot(yc_ref[0], wo_ref[2 * g:3 * g, :], preferred_element_type=F32)
    y += jnp.dot(yd_ref[0], wo_ref[3 * g:4 * g, :], preferred_element_type=F32)
    x1 = x_ref[0] + gate1_ref[...] * y
    h = _modulated_norm(x1, gain_ref[...], shift_ref[...], scale_ref[...]).astype(BF16)
    acc = jnp.zeros_like(x1)
    for j in range(0, D_FF, ff_chunk):
        u = jnp.dot(h, w1_ref[:, j:j + ff_chunk], preferred_element_type=F32)
        u = jnp.square(jnp.maximum(u, 0.0)).astype(BF16)
        acc += jnp.dot(u, w2_ref[j:j + ff_chunk, :], preferred_element_type=F32)
    o_ref[0] = x1 + gate2_ref[...] * acc


def _out_projection_mlp(x, ys, mod5, layer, gain, w_out, w1, w2, tm):
    batch, seq, d = x.shape
    g = GROUP_WIDTH
    y_spec = pl.BlockSpec((1, tm, g), lambda b, i: (b, i, 0))
    const = lambda b, i: (0, 0)
    return pl.pallas_call(
        functools.partial(_ffn_kernel, ff_chunk=512),
        grid=(batch, seq // tm),
        in_specs=[
            pl.BlockSpec((1, tm, d), lambda b, i: (b, i, 0)),
            y_spec, y_spec, y_spec, y_spec,
            _mod_spec(layer, 2),
            pl.BlockSpec((1, d), const),
            _mod_spec(layer, 3),
            _mod_spec(layer, 4),
            _mod_spec(layer, 5),
            pl.BlockSpec((d, d), const, pipeline_mode=pl.Buffered(1)),
            pl.BlockSpec((d, D_FF), const, pipeline_mode=pl.Buffered(1)),
            pl.BlockSpec((D_FF, d), const, pipeline_mode=pl.Buffered(1)),
        ],
        out_specs=pl.BlockSpec((1, tm, d), lambda b, i: (b, i, 0)),
        out_shape=jax.ShapeDtypeStruct((batch, seq, d), F32),
        compiler_params=_params(("parallel", "parallel")),
        name="out_projection_mlp",
    )(x, *ys, mod5, gain.reshape(1, d), mod5, mod5, mod5, w_out, w1, w2)


def _log_sigmoid(t):
    return jnp.minimum(t, 0.0) - jnp.log1p(jnp.exp(-jnp.abs(t)))


def _segment_rms(x, gain, ones_bd, seg):
    ss = jnp.dot((x * x).astype(BF16), ones_bd, preferred_element_type=F32)
    return x * lax.rsqrt(ss * (1.0 / seg) + NORM_EPS) * gain


def _split3_f32(x):
    hi = x.astype(BF16).astype(F32)
    r1 = x - hi
    mid = r1.astype(BF16).astype(F32)
    lo = (r1 - mid).astype(BF16).astype(F32)
    return hi, mid, lo


def _prep_kernel(slopes_ref, dq_ref, dk_ref, dv_ref, fq_ref, fk_ref, fv_ref, zg_ref,
                 dqn_ref, dkn_ref, fqn_ref, fkn_ref, gbias_ref, bd32_ref, bd64_ref,
                 dqt_o, dk_o, dvt_o, fqt_o, fk_o, fvt_o, grow_o, mcol_o, carry_ref, mcarry_ref, *, tile):
    bd32 = bd32_ref[...]
    bd64 = bd64_ref[...]
    t0 = pl.program_id(1) * tile

    @pl.when(pl.program_id(1) == 0)
    def _():
        carry_ref[...] = jnp.zeros_like(carry_ref)
        mcarry_ref[...] = jnp.zeros_like(mcarry_ref)

    t = zg_ref[0].T[0:GATE_ROWS, :] + gbias_ref[...]
    row = lax.broadcasted_iota(jnp.int32, (GATE_ROWS, tile), 0)
    lane = lax.broadcasted_iota(jnp.int32, (GATE_ROWS, tile), 1)
    whole_seq = row < ROW_ML_B
    per_chunk = (row >= ROW_ML_B) & (row < ROW_ML_I)
    acc = jnp.where(whole_seq | per_chunk, _log_sigmoid(t), 0.0)
    lane_in_chunk = lane & (CHUNK - 1)
    shift = 1
    while shift < tile:
        ok = whole_seq & (lane >= shift)
        if shift < CHUNK:
            ok = ok | (per_chunk & (lane_in_chunk >= shift))
        acc = acc + jnp.where(ok, pltpu.roll(acc, shift, 1), 0.0)
        shift *= 2
    acc = acc + jnp.where(whole_seq, carry_ref[:, 0:1], 0.0)
    carry_ref[...] = jnp.broadcast_to(acc[:, tile - 1:tile], carry_ref.shape)
    g_all = t - pltpu.roll(acc, ROW_ML_G - ROW_ML_B, 0)
    run_max = g_all
    shift = 1
    while shift < CHUNK:
        run_max = jnp.maximum(run_max, jnp.where(lane_in_chunk >= shift, pltpu.roll(run_max, shift, 1), -jnp.inf))
        shift *= 2
    grow_o[0] = jnp.where(row < ROW_ML_G, acc,
                          jnp.where(row < ROW_ML_CG, g_all, pltpu.roll(run_max, ROW_ML_CG - ROW_ML_G, 0)))

    def over_chunk(x, op):
        step = 1
        while step < CHUNK:
            x = op(x, jnp.where(lane_in_chunk < CHUNK - step, pltpu.roll(x, tile - step, 1), x))
            step *= 2
        return x

    b_g = pltpu.roll(acc, ROW_ML_G - ROW_ML_B, 0)
    b_last = over_chunk(b_g, jnp.minimum)
    g_max = over_chunk(run_max, jnp.maximum)
    a_map, d_map = b_last, g_max + b_last
    step = CHUNK
    while step < tile:
        earlier = lane >= step
        d_map = jnp.where(earlier, jnp.maximum(pltpu.roll(d_map, step, 1) + a_map, d_map), d_map)
        a_map = jnp.where(earlier, pltpu.roll(a_map, step, 1) + a_map, a_map)
        step *= 2
    m_tile = mcarry_ref[:, 0:1]
    m_after = jnp.maximum(m_tile + a_map, d_map)
    m_before = jnp.where(lane >= CHUNK, pltpu.roll(m_after, CHUNK, 1), m_tile)
    mcarry_ref[...] = jnp.broadcast_to(m_after[:, tile - 1:tile], mcarry_ref.shape)
    big_m = jnp.maximum(m_before, run_max)
    w_state = jnp.exp(m_before - big_m)
    floor = jnp.exp(-(b_g + big_m))
    w_key = jnp.exp(g_all - jnp.maximum(m_before, g_max))
    assert COL_ML_FLOOR == ROW_ML_G
    mcol_o[0] = jnp.where(
        row < COL_ML_W_STATE, pltpu.roll(big_m, GATE_ROWS + COL_ML_M - ROW_ML_G, 0),
        jnp.where(row < COL_ML_FLOOR, pltpu.roll(w_state, GATE_ROWS + COL_ML_W_STATE - ROW_ML_G, 0),
                  jnp.where(row < COL_ML_W_KEY, floor, pltpu.roll(w_key, COL_ML_W_KEY - ROW_ML_G, 0))))

    hd = HEAD_DIM
    lane_pos = (t0 + lax.broadcasted_iota(jnp.int32, (1, tile), 1)).astype(F32)
    feat_row = lax.broadcasted_iota(jnp.int32, (hd, tile), 0)
    feat_lane = lax.broadcasted_iota(jnp.int32, (tile, KEY_WIDTH), 1)
    sub_row = lax.broadcasted_iota(jnp.int32, (8, tile), 0)
    pad_rows = jnp.zeros((KEY_WIDTH - hd - 8, tile), F32)

    def bias_features(bias_row, bias_first):
        hi, mid, lo = _split3_f32(bias_row)
        parts = jnp.where(sub_row % 3 == 0, hi, jnp.where(sub_row % 3 == 1, mid, lo))
        first, second = sub_row < 3, (sub_row >= 3) & (sub_row < 6)
        in_bias, in_ones = (first, second) if bias_first else (second, first)
        return jnp.where(in_bias, parts, jnp.where(in_ones, 1.0, 0.0))

    def query_aug(bias_row):
        return jnp.concatenate([bias_features(bias_row, False), pad_rows], axis=0)

    def key_aug(k_all, h, bias_row):
        blk = k_all[:, (h // 2) * KEY_WIDTH:(h // 2 + 1) * KEY_WIDTH]
        k_head = pltpu.roll(blk, hd, 1) if h % 2 else blk
        extra = jnp.concatenate([jnp.zeros((hd, tile), F32), bias_features(bias_row, True), pad_rows], axis=0)
        return jnp.where(feat_lane < hd, k_head, extra.T)

    dq_t = (_segment_rms(dq_ref[0], dqn_ref[...], bd32, DIFF_QK_DIM) * (DIFF_QK_DIM ** -0.5 * LOG2E)).T
    dk = _segment_rms(dk_ref[0], dkn_ref[...], bd32, DIFF_QK_DIM)
    dv_t = dv_ref[0].T
    fq_t = (_segment_rms(fq_ref[0], fqn_ref[...], bd64, hd) * (hd ** -0.5 * LOG2E)).T
    fk = _segment_rms(fk_ref[0], fkn_ref[...], bd64, hd)
    fv_t = fv_ref[0].T
    ones_rows = jnp.ones((VT_ROWS - hd, tile), F32)
    for h in range(HEADS):
        rows = slice(h * hd, (h + 1) * hd)
        slope = slopes_ref[h] * LOG2E
        aug = query_aug(-slope * lane_pos)
        for comp in range(2):
            in_comp = (feat_row >= comp * DIFF_QK_DIM) & (feat_row < (comp + 1) * DIFF_QK_DIM)
            q_comp = jnp.where(in_comp, dq_t[rows], 0.0)
            dqt_o[0, h, comp] = jnp.concatenate([q_comp, aug], axis=0).astype(BF16)
        dk_o[0, h] = key_aug(dk, h, slope * lane_pos).astype(BF16)
        dvt_o[0, h] = jnp.concatenate([dv_t[rows], ones_rows], axis=0).astype(BF16)
        f_row = acc[ROW_FOX_F + h:ROW_FOX_F + h + 1, :] * LOG2E
        fqt_o[0, h] = jnp.concatenate([fq_t[rows], query_aug(f_row)], axis=0).astype(BF16)
        fk_o[0, h] = key_aug(fk, h, -f_row).astype(BF16)
        fvt_o[0, h] = jnp.concatenate([fv_t[rows], ones_rows], axis=0).astype(BF16)


def _attention_prep(z, slopes, layer_params, tile):
    batch, seq, _ = z.shape
    g = GROUP_WIDTH
    dqn, dkn, fqn, fkn, gbias, bd32, bd64 = layer_params

    def zspec(block):
        return pl.BlockSpec((1, tile, g), lambda b, i: (b, i, block))

    const = lambda b, i: (0, 0)
    qt_spec = pl.BlockSpec((1, HEADS, KEY_WIDTH, tile), lambda b, i: (b, 0, 0, i))
    qt2_spec = pl.BlockSpec((1, HEADS, 2, KEY_WIDTH, tile), lambda b, i: (b, 0, 0, 0, i))
    k_spec = pl.BlockSpec((1, HEADS, tile, KEY_WIDTH), lambda b, i: (b, 0, i, 0))
    vt_spec = pl.BlockSpec((1, HEADS, VT_ROWS, tile), lambda b, i: (b, 0, 0, i))
    qt_shape = jax.ShapeDtypeStruct((batch, HEADS, KEY_WIDTH, seq), BF16)
    qt2_shape = jax.ShapeDtypeStruct((batch, HEADS, 2, KEY_WIDTH, seq), BF16)
    k_shape = jax.ShapeDtypeStruct((batch, HEADS, seq, KEY_WIDTH), BF16)
    vt_shape = jax.ShapeDtypeStruct((batch, HEADS, VT_ROWS, seq), BF16)
    return pl.pallas_call(
        functools.partial(_prep_kernel, tile=tile),
        grid=(batch, seq // tile),
        in_specs=[
            pl.BlockSpec(memory_space=pltpu.SMEM),
            zspec(Z_DQ), zspec(Z_DK), zspec(Z_DV), zspec(Z_FQ), zspec(Z_FK), zspec(Z_FV),
            pl.BlockSpec((1, tile, GATE_LANES), lambda b, i: (b, i, Z_GATE_BLOCK)),
            pl.BlockSpec((1, g), const), pl.BlockSpec((1, g), const),
            pl.BlockSpec((1, g), const), pl.BlockSpec((1, g), const),
            pl.BlockSpec((GATE_ROWS, 1), const),
            pl.BlockSpec((g, g), const), pl.BlockSpec((g, g), const),
        ],
        out_specs=[qt2_spec, k_spec, vt_spec, qt_spec, k_spec, vt_spec,
                   pl.BlockSpec((1, GATE_ROWS, tile), lambda b, i: (b, 0, i)),
                   pl.BlockSpec((1, GATE_ROWS, tile), lambda b, i: (b, 0, i))],
        out_shape=[qt2_shape, k_shape, vt_shape, qt_shape, k_shape, vt_shape,
                   jax.ShapeDtypeStruct((batch, GATE_ROWS, seq), F32),
                   jax.ShapeDtypeStruct((batch, GATE_ROWS, seq), F32)],
        scratch_shapes=[pltpu.VMEM((GATE_ROWS, GATE_LANES), F32), pltpu.VMEM((GATE_ROWS, GATE_LANES), F32)],
        compiler_params=_params(("parallel", "arbitrary")),
        name="attention_prep",
    )(slopes, z, z, z, z, z, z, z, dqn, dkn, fqn, fkn, gbias, bd32, bd64)


def _attention_sweep(qt_of_head, k_ref, vt_ref, qi, tq, tk, ncomp, scratch):
    cols = ncomp * tq
    assert tq % tk == 0
    s_scr, p_scr, m_scr, a_scr, acc_scr = scratch
    m_scr[...] = jnp.full(m_scr.shape, -jnp.inf, F32)
    a_scr[...] = jnp.ones(a_scr.shape, F32)
    acc_scr[...] = jnp.zeros(acc_scr.shape, F32)
    p_scr[1] = jnp.zeros(p_scr.shape[1:], BF16)
    per_block = tq // tk
    n_full = per_block * qi
    every = [(0, cols)]
    heads_in_flight = 1

    def visible(d):
        return [(c * tq + d * tk, (c + 1) * tq) for c in range(ncomp)]

    def logits(h, j, ranges):
        start = pl.multiple_of(j * tk, tk)
        k = k_ref[0, h, pl.ds(start, tk), :]
        qt = qt_of_head(h)
        return [jnp.dot(k, qt[:, lo:hi], preferred_element_type=F32) for lo, hi in ranges]

    def weighted_values(h, j, slot, ranges):
        start = pl.multiple_of(j * tk, tk)
        vt = vt_ref[0, h, :, pl.ds(start, tk)]
        return [jnp.dot(vt, p_scr[slot, h, :, lo:hi], preferred_element_type=F32) for lo, hi in ranges]

    def accumulate(pv, ranges):
        for h, parts in pv.items():
            for (lo, hi), x in zip(ranges, parts):
                acc_scr[h, :, lo:hi] = a_scr[h, :, lo:hi] * acc_scr[h, :, lo:hi] + x

    def stage(j, slot, prev_j, prev, cur, nxt, masked):
        for first in range(0, HEADS, heads_in_flight):
            group = range(first, first + heads_in_flight)
            pv = {h: weighted_values(h, prev_j, 1 - slot, prev) for h in group}
            s_next = {h: logits(h, j + 1, nxt) for h in group} if nxt else None
            alphas = {}
            for h in group:
                alphas[h] = []
                for lo, hi in cur:
                    s = s_scr[slot, h, :, lo:hi]
                    if masked:
                        key = lax.broadcasted_iota(jnp.int32, (tk, hi - lo), 0)
                        query = (lax.broadcasted_iota(jnp.int32, (tk, hi - lo), 1) + lo) & (tq - 1)
                        s = jnp.where(key - query <= qi * tq - j * tk, s, -jnp.inf)
                    m_prev = m_scr[h, :, lo:hi]
                    m_new = jnp.maximum(m_prev, jnp.max(s, axis=0, keepdims=True))
                    alphas[h].append(jnp.exp2(m_prev - m_new))
                    p_scr[slot, h, :, lo:hi] = jnp.exp2(s - m_new).astype(BF16)
                    m_scr[h, :, lo:hi] = m_new
            accumulate(pv, prev)
            for h in group:
                for (lo, hi), alpha in zip(cur, alphas[h]):
                    a_scr[h, :, lo:hi] = alpha
                if nxt:
                    for (lo, hi), x in zip(nxt, s_next[h]):
                        s_scr[1 - slot, h, :, lo:hi] = x

    for h in range(HEADS):
        s_scr[0, h] = logits(h, 0, every)[0]

    def slot_of(j, d):
        return d % 2 if per_block % 2 == 0 else j & 1

    def full_block(i, carry):
        for d in range(per_block):
            j = i * per_block + d
            stage(j, slot_of(j, d), jnp.maximum(j - 1, 0), every, every, every, False)
        return carry

    lax.fori_loop(0, qi, full_block, 0)
    for d in range(per_block):
        j = n_full + d
        stage(j, slot_of(j, d), jnp.maximum(j - 1, 0), visible(d - 1) if d else every, visible(d),
              visible(d + 1) if d + 1 < per_block else None, True)
    j_last = n_full + per_block - 1
    last = visible(per_block - 1)
    accumulate({h: weighted_values(h, j_last, slot_of(j_last, per_block - 1), last) for h in range(HEADS)}, last)


def _diff_attn_kernel(lam_ref, qt_ref, k_ref, vt_ref, gain_ref, bd64_ref, o_ref, ot_scr, *scratch,
                      tq, tk, out_scale):
    qi = pl.program_id(1)
    lam = lam_ref[0]

    def qt_of_head(h):
        return jnp.concatenate([qt_ref[0, h, 0], qt_ref[0, h, 1]], axis=1)

    _attention_sweep(qt_of_head, k_ref, vt_ref, qi, tq, tk, 2, scratch)
    for h in range(HEADS):
        o = _normalised_output(scratch[-1], h)
        ot_scr[h * HEAD_DIM:(h + 1) * HEAD_DIM, :] = o[:, 0:tq] - lam * o[:, tq:2 * tq]
    y = _segment_rms(ot_scr[...].T, gain_ref[...] * out_scale, bd64_ref[...], HEAD_DIM)
    o_ref[0] = y.astype(BF16)


def _fox_attn_kernel(qt_ref, k_ref, vt_ref, g_ref, o_ref, ot_scr, *scratch, tq, tk):
    qi = pl.program_id(1)
    _attention_sweep(lambda h: qt_ref[0, h], k_ref, vt_ref, qi, tq, tk, 1, scratch)
    for h in range(HEADS):
        ot_scr[h * HEAD_DIM:(h + 1) * HEAD_DIM, :] = _normalised_output(scratch[-1], h)
    o_ref[0] = (ot_scr[...].T * jax.nn.sigmoid(g_ref[0])).astype(BF16)


def _attn_scratch(tq, tk, ncomp):
    cols = ncomp * tq
    return [
        pltpu.VMEM((GROUP_WIDTH, tq), F32),
        pltpu.VMEM((2, HEADS, tk, cols), F32),
        pltpu.VMEM((2, HEADS, tk, cols), BF16),
        pltpu.VMEM((HEADS, 1, cols), F32),
        pltpu.VMEM((HEADS, 1, cols), F32),
        pltpu.VMEM((HEADS, VT_ROWS, cols), F32),
    ]


def _normalised_output(acc_scr, h):
    return acc_scr[h, 0:HEAD_DIM, :] / acc_scr[h, HEAD_DIM:HEAD_DIM + 1, :]


def _diff_attention(qt, k, vt, lam, gain, bd64, out_scale, tq, tk):
    batch, _, seq, _ = k.shape
    g = GROUP_WIDTH
    return pl.pallas_call(
        functools.partial(_diff_attn_kernel, tq=tq, tk=tk, out_scale=out_scale),
        grid=(batch, seq // tq),
        in_specs=[
            pl.BlockSpec(memory_space=pltpu.SMEM),
            pl.BlockSpec((1, HEADS, 2, KEY_WIDTH, tq), lambda b, i: (b, 0, 0, 0, i)),
            pl.BlockSpec((1, HEADS, seq, KEY_WIDTH), lambda b, i: (b, 0, 0, 0)),
            pl.BlockSpec((1, HEADS, VT_ROWS, seq), lambda b, i: (b, 0, 0, 0)),
            pl.BlockSpec((1, g), lambda b, i: (0, 0)),
            pl.BlockSpec((g, g), lambda b, i: (0, 0)),
        ],
        out_specs=pl.BlockSpec((1, tq, g), lambda b, i: (b, i, 0)),
        out_shape=jax.ShapeDtypeStruct((batch, seq, g), BF16),
        scratch_shapes=_attn_scratch(tq, tk, 2),
        compiler_params=_params(("parallel", "arbitrary")),
        name="diff_attention",
    )(lam, qt, k, vt, gain, bd64)


def _fox_attention(qt, k, vt, z, tq, tk):
    batch, _, seq, _ = k.shape
    g = GROUP_WIDTH
    return pl.pallas_call(
        functools.partial(_fox_attn_kernel, tq=tq, tk=tk),
        grid=(batch, seq // tq),
        in_specs=[
            pl.BlockSpec((1, HEADS, KEY_WIDTH, tq), lambda b, i: (b, 0, 0, i)),
            pl.BlockSpec((1, HEADS, seq, KEY_WIDTH), lambda b, i: (b, 0, 0, 0)),
            pl.BlockSpec((1, HEADS, VT_ROWS, seq), lambda b, i: (b, 0, 0, 0)),
            pl.BlockSpec((1, tq, g), lambda b, i: (b, i, Z_FG)),
        ],
        out_specs=pl.BlockSpec((1, tq, g), lambda b, i: (b, i, 0)),
        out_shape=jax.ShapeDtypeStruct((batch, seq, g), BF16),
        scratch_shapes=_attn_scratch(tq, tk, 1),
        compiler_params=_params(("parallel", "arbitrary")),
        name="fox_attention",
    )(qt, k, vt, z)


def _head_masks(rows):
    lane_head = lax.broadcasted_iota(jnp.int32, (rows, GROUP_WIDTH), 1) // HEAD_DIM
    return [lane_head == h for h in range(HEADS)]


def _stack_heads(x, masks):
    return jnp.concatenate([jnp.where(m, x, 0.0) for m in masks], axis=0).astype(BF16)


def _hgrn2_kernel(q_ref, f_ref, i_ref, g_ref, lb_ref, gain_ref, bd64_ref, tri_ref, mask_ref,
                  o_ref, st_ref, *, chunks):
    @pl.when(pl.program_id(1) == 0)
    def _():
        st_ref[...] = jnp.zeros_like(st_ref)

    lb = lb_ref[...]
    bd64 = bd64_ref[...]
    tri = tri_ref[...]
    sub = 8
    gw = GROUP_WIDTH
    groups = CHUNK // sub
    sub_idx = lax.broadcasted_iota(jnp.int32, (groups, sub, gw), 1)
    row = lax.broadcasted_iota(jnp.int32, (CHUNK, gw), 0)
    key_pos = lax.broadcasted_iota(jnp.int32, (CHUNK, gw), 1) % HEAD_DIM
    head_masks = _head_masks(CHUNK)
    stack_heads = functools.partial(_stack_heads, masks=head_masks)

    span = range(chunks)
    rows = [slice(c * CHUNK, (c + 1) * CHUNK) for c in span]
    q = [q_ref[0, r, :] for r in rows]
    v = [i_ref[0, r, :] for r in rows]
    kk, b = [], []
    for c in span:
        f = lb + (1.0 - lb) * jax.nn.sigmoid(f_ref[0, rows[c], :])
        kk.append(1.0 - f)
        hi, mid, lo = _split3(jnp.log(f) * LOG2E)
        b.append(jnp.dot(tri, hi, preferred_element_type=F32)
                 + jnp.dot(tri, mid, preferred_element_type=F32)
                 + jnp.dot(tri, lo, preferred_element_type=F32))

    o = []
    for c in span:
        q3, k3, b3, v3 = (t.reshape(groups, sub, gw) for t in (q[c], kk[c], b[c], v[c]))
        o3 = jnp.zeros((groups, sub, gw), F32)
        for lag in range(sub):
            if lag == 0:
                x = q3 * k3
                vd = v3
            else:
                kd = pltpu.roll(k3, lag, 1)
                bd = pltpu.roll(b3, lag, 1)
                vd = pltpu.roll(v3, lag, 1)
                x = jnp.where(sub_idx >= lag, q3 * kd * jnp.exp2(b3 - bd), 0.0)
            a = jnp.dot(x.reshape(CHUNK, gw).astype(BF16), bd64, preferred_element_type=F32)
            o3 = o3 + a.reshape(groups, sub, gw) * vd
        o.append(o3.reshape(CHUNK, gw))

    a_far = []
    for c in span:
        total = None
        for half in (sub, 2 * sub, 4 * sub):
            block = 2 * half
            ref_rows = jnp.concatenate(
                [jnp.broadcast_to(b[c][p + half - 1:p + half, :], (block, gw)) for p in range(0, CHUNK, block)],
                axis=0)
            upper = ((row // half) & 1) == 1
            e = jnp.exp2(jnp.where(upper, b[c] - ref_rows, ref_rows - b[c]))
            qs = jnp.where(upper, q[c] * e, 0.0).astype(BF16)
            ks = stack_heads(jnp.where(upper, 0.0, kk[c] * e))
            a_lvl = lax.dot_general(qs, ks, (((1,), (1,)), ((), ())), preferred_element_type=F32)
            if block < CHUNK:
                a_lvl = jnp.where(row // block == key_pos // block, a_lvl, 0.0)
            total = a_lvl if total is None else total + a_lvl
        a_far.append(total.astype(BF16))
    for c in span:
        o[c] = o[c] + jnp.dot(a_far[c], stack_heads(v[c]), preferred_element_type=F32)

    upd = []
    for c in span:
        kp = (kk[c] * jnp.exp2(b[c][CHUNK - 1:CHUNK, :] - b[c])).astype(BF16)
        upd.append(lax.dot_general(v[c].astype(BF16), kp, (((0,), (0,)), ((), ())),
                                   preferred_element_type=F32))
    st = st_ref[...]
    for c in span:
        qe = (q[c] * jnp.exp2(b[c])).astype(BF16)
        o[c] = o[c] + lax.dot_general(qe, st.astype(BF16), (((1,), (1,)), ((), ())),
                                      preferred_element_type=F32)
        st = mask_ref[...] * (st * jnp.exp2(b[c][CHUNK - 1:CHUNK, :]) + upd[c])
    st_ref[...] = st

    for c in span:
        y = _segment_rms(o[c], gain_ref[...], bd64, HEAD_DIM)
        g = g_ref[0, rows[c], :]
        o_ref[0, rows[c], :] = (y * (g * jax.nn.sigmoid(g))).astype(BF16)


def _hgrn2(z, lb, gain, bd64, tri, mask, tile):
    batch, seq, _ = z.shape
    g = GROUP_WIDTH

    def zspec(block):
        return pl.BlockSpec((1, tile, g), lambda b, i: (b, i, block))

    const = lambda b, i: (0, 0)
    return pl.pallas_call(
        functools.partial(_hgrn2_kernel, chunks=tile // CHUNK),
        grid=(batch, seq // tile),
        in_specs=[
            zspec(Z_HQ), zspec(Z_HF), zspec(Z_HI), zspec(Z_HG),
            pl.BlockSpec((1, g), const), pl.BlockSpec((1, g), const),
            pl.BlockSpec((g, g), const), pl.BlockSpec((CHUNK, CHUNK), const),
            pl.BlockSpec((g, g), const),
        ],
        out_specs=pl.BlockSpec((1, tile, g), lambda b, i: (b, i, 0)),
        out_shape=jax.ShapeDtypeStruct((batch, seq, g), BF16),
        scratch_shapes=[pltpu.VMEM((g, g), F32)],
        compiler_params=_params(("parallel", "arbitrary")),
        name="hgrn2",
    )(z, z, z, z, lb, gain, bd64, tri, mask)


def _mlstm_kernel(q_ref, k_ref, v_ref, og_ref, gcol_ref, grow_ref, conv_ref, expand_ref, bd64_ref,
                  mask_ref, y_ref, prev_scr, c_scr, n_scr, *, tile):
    @pl.when(pl.program_id(1) == 0)
    def _():
        prev_scr[...] = jnp.zeros_like(prev_scr)
        c_scr[...] = jnp.zeros_like(c_scr)
        n_scr[...] = jnp.zeros_like(n_scr)

    g = GROUP_WIDTH
    halo = prev_scr.shape[0]
    raw = jnp.concatenate([q_ref[0], k_ref[0]], axis=1)
    padded = jnp.concatenate([prev_scr[...], raw], axis=0)
    w = conv_ref[...]
    conv = w[CONV_WIDTH - 1:CONV_WIDTH, :] * raw
    for back in range(1, CONV_WIDTH):
        tap = CONV_WIDTH - 1 - back
        conv = conv + w[tap:tap + 1, :] * pltpu.roll(padded, back, 0)[halo:]
    prev_scr[...] = raw[tile - halo:]
    act = conv * jax.nn.sigmoid(conv)
    qc = act[:, :g] * HEAD_DIM ** -0.5
    kc = act[:, g:]
    v = v_ref[0]

    bd64 = bd64_ref[...]
    expand = expand_ref[...]
    chunks = tile // CHUNK
    span = range(chunks)
    rows = [slice(c * CHUNK, (c + 1) * CHUNK) for c in span]
    t_idx = lax.broadcasted_iota(jnp.int32, (CHUNK, g), 0)
    s_idx = lax.broadcasted_iota(jnp.int32, (CHUNK, g), 1) % HEAD_DIM
    causal = s_idx <= t_idx

    m_b, w_state_b, floor_b, w_key_b, g_b = [], [], [], [], []
    for c in span:
        wide = sum(lax.dot_general(part, expand, (((0,), (0,)), ((), ())), preferred_element_type=F32)
                   for part in _split3(gcol_ref[0, :, rows[c]]))
        for dst, col in ((m_b, COL_ML_M), (w_state_b, COL_ML_W_STATE), (floor_b, COL_ML_FLOOR),
                         (w_key_b, COL_ML_W_KEY)):
            dst.append(wide[:, col // HEADS * g:(col // HEADS + 1) * g])
        g_b.append(jnp.concatenate(
            [grow_ref[0, ROW_ML_G + h:ROW_ML_G + h + 1, rows[c]] for h in range(HEADS)], axis=1))

    qb = [qc[r].astype(BF16) for r in rows]
    head_masks = _head_masks(CHUNK)
    v_stack = [_stack_heads(v[r], head_masks) for r in rows]
    scores = [lax.dot_general(qb[c], _stack_heads(kc[rows[c]], head_masks), (((1,), (1,)), ((), ())),
                              preferred_element_type=F32) for c in span]
    num_intra, den_intra = [], []
    for c in span:
        s = scores[c] * jnp.exp(jnp.where(causal, g_b[c] - m_b[c], -jnp.inf))
        num_intra.append(jnp.dot(s.astype(BF16), v_stack[c], preferred_element_type=F32))
        den_intra.append(_segment_sum(s, bd64))
    c_upd, n_upd = [], []
    for c in span:
        kw = kc[rows[c]] * w_key_b[c]
        c_upd.append(mask_ref[...] * lax.dot_general(kw.astype(BF16), v[rows[c]].astype(BF16),
                                                     (((0,), (0,)), ((), ())), preferred_element_type=F32))
        n_upd.append(jnp.sum(kw, axis=0, keepdims=True))

    cm = c_scr[...]
    nv = n_scr[...]
    for c in span:
        inter = jnp.dot(qb[c], cm.astype(BF16), preferred_element_type=F32)
        q_dot_n = _segment_sum(qc[rows[c]] * nv, bd64)
        num = num_intra[c] + w_state_b[c] * inter
        den = den_intra[c] + w_state_b[c] * q_dot_n
        y = num / jnp.maximum(jnp.abs(den), floor_b[c])
        y_ref[0, rows[c], :] = (jax.nn.sigmoid(og_ref[0, rows[c], :]) * y).astype(BF16)
        decay = w_state_b[c][CHUNK - 1:CHUNK, :]
        cm = decay * cm + c_upd[c]
        nv = decay * nv + n_upd[c]
    c_scr[...] = cm
    n_scr[...] = nv


def _mlstm(z, gcol, grow, conv_w, bd64, mask, tile):
    batch, seq, _ = z.shape
    g = GROUP_WIDTH
    expand = jnp.asarray(np.repeat(np.eye(GATE_ROWS, dtype=np.float32), HEAD_DIM, axis=1), dtype=BF16)

    def zspec(block):
        return pl.BlockSpec((1, tile, g), lambda b, i: (b, i, block))

    const = lambda b, i: (0, 0)
    return pl.pallas_call(
        functools.partial(_mlstm_kernel, tile=tile),
        grid=(batch, seq // tile),
        in_specs=[
            zspec(Z_MQ), zspec(Z_MK), zspec(Z_MV), zspec(Z_MO),
            pl.BlockSpec((1, GATE_ROWS, tile), lambda b, i: (b, 0, i)),
            pl.BlockSpec((1, GATE_ROWS, tile), lambda b, i: (b, 0, i)),
            pl.BlockSpec((CONV_WIDTH, 2 * g), const),
            pl.BlockSpec((GATE_ROWS, GATE_ROWS * HEAD_DIM), const),
            pl.BlockSpec((g, g), const),
            pl.BlockSpec((g, g), const),
        ],
        out_specs=pl.BlockSpec((1, tile, g), lambda b, i: (b, i, 0)),
        out_shape=jax.ShapeDtypeStruct((batch, seq, g), BF16),
        scratch_shapes=[
            pltpu.VMEM((8, 2 * g), F32),
            pltpu.VMEM((g, g), F32),
            pltpu.VMEM((1, g), F32),
        ],
        compiler_params=_params(("parallel", "arbitrary")),
        name="mlstm",
    )(z, z, z, z, gcol, grow, conv_w, expand, bd64, mask)


def _tile_rows(v, reps):
    return jnp.tile(v.astype(F32), reps).reshape(1, -1)


def kernel(x, c, w_ada, b_ada, norm_mix_gain, norm_ff_gain, w_in, w_out, hg_lb_logits, hg_norm_gain,
           diff_qn_gain, diff_kn_gain, diff_lambda, diff_sub_gain, fox_qn_gain, fox_kn_gain, fox_f_bias,
           mlstm_conv, mlstm_i_bias, mlstm_f_bias, w_ff1, w_ff2):
    depth = w_in.shape[0]
    batch, seq, d = x.shape
    g, hd = GROUP_WIDTH, HEAD_DIM
    tm = min(512, seq)
    tq = min(512, seq)
    tk = 128
    tr = min(512, seq)
    tp = min(1024, seq)

    p_lb = jax.nn.softmax(hg_lb_logits.astype(F32), axis=0)
    lower_bounds = jnp.cumsum(p_lb, axis=0) - p_lb[0:1]
    bd32 = _block_diag_ones(g, DIFF_QK_DIM)
    bd64 = _block_diag_ones(g, hd)
    bd64_f32 = bd64.astype(F32)
    tri = jnp.asarray(np.tril(np.ones((CHUNK, CHUNK), np.float32)), dtype=BF16)
    slopes = jnp.asarray(2.0 ** (-8.0 * np.arange(1, HEADS + 1) / HEADS), F32)

    mod = _ada_modulation(c.astype(F32), w_ada, b_ada)
    mod5 = mod.reshape(depth, batch, 6, 1, d)

    for l in range(depth):
        w_perm = _permuted_in_weights(w_in[l])
        z = _input_projection(x, mod5, l, norm_mix_gain[l], w_perm, tm)

        gbias = jnp.zeros((GATE_ROWS,), F32)
        gbias = gbias.at[ROW_FOX_F:ROW_FOX_F + HEADS].set(fox_f_bias[l].astype(F32))
        gbias = gbias.at[ROW_ML_B:ROW_ML_B + HEADS].set(mlstm_f_bias[l].astype(F32))
        gbias = gbias.at[ROW_ML_I:ROW_ML_I + HEADS].set(mlstm_i_bias[l].astype(F32))
        prep_params = (
            _tile_rows(diff_qn_gain[l], 2 * HEADS), _tile_rows(diff_kn_gain[l], 2 * HEADS),
            _tile_rows(fox_qn_gain[l], HEADS), _tile_rows(fox_kn_gain[l], HEADS),
            gbias.reshape(GATE_ROWS, 1), bd32, bd64,
        )
        dqt, dk, dvt, fqt, fk, fvt, grow, mcol_t = _attention_prep(z, slopes, prep_params, tp)

        lam_init = 0.8 - 0.6 * math.exp(-0.3 * l)
        lv = diff_lambda[l].astype(F32)
        lam = jnp.exp(jnp.sum(lv[0] * lv[1])) - jnp.exp(jnp.sum(lv[2] * lv[3])) + lam_init

        y_a = _hgrn2(z, lower_bounds[l].reshape(1, g), _tile_rows(hg_norm_gain[l], HEADS),
                     bd64, tri, bd64_f32, tr)
        y_b = _diff_attention(dqt, dk, dvt, lam.reshape(1), _tile_rows(diff_sub_gain[l], HEADS),
                              bd64, 1.0 - lam_init, tq, tk)
        y_c = _fox_attention(fqt, fk, fvt, z, tq, tk)
        y_d = _mlstm(z, mcol_t, grow, mlstm_conv[l].astype(F32), bd64, bd64_f32, tr)

        x = _out_projection_mlp(x, (y_a, y_b, y_c, y_d), mod5, l, norm_ff_gain[l],
                                w_out[l].astype(BF16), w_ff1[l].astype(BF16), w_ff2[l].astype(BF16), tm)
    return x
```
